```python
import jax, jax.numpy as jnp
from jax import lax
import numpy as np

D_MODEL = 2048
BATCH = 4
SEQ = 4096
DEPTH = 1

N_MEM = 256
D_MIX = D_MODEL
D_DELTA = D_MIX // 2
D_CONV = D_MIX - D_DELTA
DN_HEADS = 8
DN_HEAD_DIM = D_DELTA // DN_HEADS
DN_CONV_W = 4
DN_CHUNK = 64
CF_CONV_W = 31
XA_HEADS = 4
XA_HEAD_DIM = D_MODEL // XA_HEADS
D_FF = 4 * D_MODEL
EPS = 1e-6
D_IN = 4 * D_DELTA + 2 * DN_HEADS + 2 * D_CONV

kernel_name = "hybrid_gdn_conformer_block"


def rms_norm(x, g):
    xf = x.astype(jnp.float32)
    y = xf * lax.rsqrt(jnp.mean(xf * xf, axis=-1, keepdims=True) + EPS)
    return (y * g.astype(jnp.float32)).astype(x.dtype)


def layer_norm(x, g, b):
    xf = x.astype(jnp.float32)
    mu = jnp.mean(xf, axis=-1, keepdims=True)
    xc = xf - mu
    var = jnp.mean(xc * xc, axis=-1, keepdims=True)
    y = xc * lax.rsqrt(var + EPS) * g.astype(jnp.float32) + b.astype(jnp.float32)
    return y.astype(x.dtype)


def l2_normalize(x):
    xf = x.astype(jnp.float32)
    return xf * lax.rsqrt(jnp.sum(xf * xf, axis=-1, keepdims=True) + EPS)


def causal_depthwise_conv(x, w):
    width, ch = w.shape
    return lax.conv_general_dilated(
        x, w[:, None, :].astype(x.dtype), window_strides=(1,), padding=[(width - 1, 0)],
        dimension_numbers=("NWC", "WIO", "NWC"), feature_group_count=ch)


def chunk_gated_delta_rule(q, k, v, g, beta):
    B, T, H, Dk = q.shape
    Dv = v.shape[-1]
    C = DN_CHUNK
    N = T // C

    def to_chunks(t):
        t = t.reshape((B, N, C, H) + t.shape[3:])
        return jnp.moveaxis(t, 3, 1)

    q, k, v, g, beta = (to_chunks(t) for t in (q, k, v, g, beta))
    q = q * (Dk ** -0.5)
    g = jnp.cumsum(g, axis=-1)
    causal = jnp.tril(jnp.ones((C, C), dtype=bool))
    strict = jnp.tril(jnp.ones((C, C), dtype=bool), -1)
    decay = jnp.exp(jnp.where(causal, g[..., :, None] - g[..., None, :], -jnp.inf))

    k_beta = k * beta[..., None]
    v_beta = v * beta[..., None]
    a = jnp.einsum("bhncd,bhnsd->bhncs", k_beta, k) * decay
    t_mat = jnp.eye(C, dtype=jnp.float32) + jnp.where(strict, a, 0.0)
    rhs = jnp.concatenate([v_beta, k_beta * jnp.exp(g)[..., None]], axis=-1)
    sol = lax.linalg.triangular_solve(t_mat, rhs, left_side=True, lower=True, unit_diagonal=True)
    u, w = sol[..., :Dv], sol[..., Dv:]

    attn = jnp.einsum("bhncd,bhnsd->bhncs", q, k) * decay
    q_dec = q * jnp.exp(g)[..., None]
    g_last = g[..., -1]
    k_dec = k * jnp.exp(g_last[..., None] - g)[..., None]

    def step(S, xs):
        q_c, k_c, u_c, w_c, attn_c, gl = xs
        v_new = u_c - jnp.einsum("bhcd,bhde->bhce", w_c, S)
        o = jnp.einsum("bhcd,bhde->bhce", q_c, S) + jnp.einsum("bhcs,bhse->bhce", attn_c, v_new)
        S = S * jnp.exp(gl)[..., None, None] + jnp.einsum("bhcd,bhce->bhde", k_c, v_new)
        return S, o

    xs = tuple(jnp.moveaxis(t, 2, 0) for t in (q_dec, k_dec, u, w, attn, g_last))
    S0 = jnp.zeros((B, H, Dk, Dv), dtype=jnp.float32)
    _, o = lax.scan(step, S0, xs)
    return jnp.transpose(o, (1, 0, 3, 2, 4)).reshape(B, T, H, Dv)


def hybrid_mixer(xn, w_in, dn_conv_w, dn_a_log, dn_dt_bias, dn_norm_g,
                 cf_dw_w, cf_dw_b, cf_ln_g, cf_ln_b, w_out):
    B, T, _ = xn.shape
    proj = xn @ w_in
    qkv, z, b_raw, a_raw, glu = jnp.split(
        proj, [3 * D_DELTA, 4 * D_DELTA, 4 * D_DELTA + DN_HEADS, 4 * D_DELTA + 2 * DN_HEADS], axis=-1)

    qkv = jax.nn.silu(causal_depthwise_conv(qkv, dn_conv_w))
    q, k, v = jnp.split(qkv, 3, axis=-1)
    hd = (B, T, DN_HEADS, DN_HEAD_DIM)
    q = l2_normalize(q.reshape(hd))
    k = l2_normalize(k.reshape(hd))
    v = v.reshape(hd).astype(jnp.float32)
    beta = jax.nn.sigmoid(b_raw.astype(jnp.float32))
    g = -jnp.exp(dn_a_log.astype(jnp.float32)) * jax.nn.softplus(
        a_raw.astype(jnp.float32) + dn_dt_bias.astype(jnp.float32))
    o = chunk_gated_delta_rule(q, k, v, g, beta)
    o = rms_norm(o, dn_norm_g) * jax.nn.silu(z.reshape(hd).astype(jnp.float32))
    o = o.reshape(B, T, D_DELTA).astype(xn.dtype)

    c = glu[..., :D_CONV] * jax.nn.sigmoid(glu[..., D_CONV:])
    c = causal_depthwise_conv(c, cf_dw_w) + cf_dw_b
    c = jax.nn.silu(layer_norm(c, cf_ln_g, cf_ln_b))

    return jnp.concatenate([o, c], axis=-1) @ w_out


def memory_cross_attention(hn, mem_n, w_q, w_k, w_v, w_o):
    B, T, _ = hn.shape
    M = mem_n.shape[1]
    q = (hn @ w_q).reshape(B, T, XA_HEADS, XA_HEAD_DIM)
    k = (mem_n @ w_k).reshape(B, M, XA_HEADS, XA_HEAD_DIM)
    v = (mem_n @ w_v).reshape(B, M, XA_HEADS, XA_HEAD_DIM)
    s = jnp.einsum("bthd,bmhd->bhtm", q, k).astype(jnp.float32) * (XA_HEAD_DIM ** -0.5)
    p = jax.nn.softmax(s, axis=-1).astype(v.dtype)
    o = jnp.einsum("bhtm,bmhd->bthd", p, v).reshape(B, T, D_MODEL)
    return o @ w_o


def squared_relu_mlp(hn, w1, w2):
    return jnp.square(jax.nn.relu(hn @ w1)) @ w2


def setup_inputs(seed: int = 0) -> dict:
    key = jax.random.key(seed)
    ks = jax.random.split(key, 24)
    f32 = jnp.float32
    L = DEPTH

    def dense(k, fan_in, fan_out):
        return jax.random.normal(k, (L, fan_in, fan_out), f32) * fan_in ** -0.5

    def gain(k, n):
        return 1.0 + 0.02 * jax.random.normal(k, (L, n), f32)

    dt = jnp.exp(jax.random.uniform(ks[6], (L, DN_HEADS), f32, np.log(1e-3), np.log(1e-1)))
    return {
        "x": jax.random.normal(ks[0], (BATCH, SEQ, D_MODEL), f32),
        "mem": jax.random.normal(ks[1], (BATCH, N_MEM, D_MODEL), f32),
        "norm_mix_g": gain(ks[2], D_MODEL),
        "w_in": dense(ks[3], D_MODEL, D_IN),
        "dn_conv_w": jax.random.normal(ks[4], (L, DN_CONV_W, 3 * D_DELTA), f32) * DN_CONV_W ** -0.5,
        "dn_a_log": jnp.log(jax.random.uniform(ks[5], (L, DN_HEADS), f32, 1.0, 16.0)),
        "dn_dt_bias": dt + jnp.log(-jnp.expm1(-dt)),
        "dn_norm_g": gain(ks[7], DN_HEAD_DIM),
        "cf_dw_w": jax.random.normal(ks[8], (L, CF_CONV_W, D_CONV), f32) * CF_CONV_W ** -0.5,
        "cf_dw_b": 0.02 * jax.random.normal(ks[9], (L, D_CONV), f32),
        "cf_ln_g": gain(ks[10], D_CONV),
        "cf_ln_b": 0.02 * jax.random.normal(ks[11], (L, D_CONV), f32),
        "w_out": dense(ks[12], D_MIX, D_MODEL),
        "norm_xa_g": gain(ks[13], D_MODEL),
        "norm_mem_g": gain(ks[14], D_MODEL),
        "xa_wq": dense(ks[15], D_MODEL, D_MODEL),
        "xa_wk": dense(ks[16], D_MODEL, D_MODEL),
        "xa_wv": dense(ks[17], D_MODEL, D_MODEL),
        "xa_wo": dense(ks[18], D_MODEL, D_MODEL),
        "norm_mlp_g": gain(ks[19], D_MODEL),
        "mlp_w1": dense(ks[20], D_MODEL, D_FF),
        "mlp_w2": dense(ks[21], D_FF, D_MODEL),
        "norm_final_g": 1.0 + 0.02 * jax.random.normal(ks[22], (D_MODEL,), f32),
    }


def reference(x, mem, norm_mix_g, w_in, dn_conv_w, dn_a_log, dn_dt_bias, dn_norm_g,
              cf_dw_w, cf_dw_b, cf_ln_g, cf_ln_b, w_out, norm_xa_g, norm_mem_g,
              xa_wq, xa_wk, xa_wv, xa_wo, norm_mlp_g, mlp_w1, mlp_w2, norm_final_g):
    h = x
    for i in range(DEPTH):
        h = h + hybrid_mixer(rms_norm(h, norm_mix_g[i]), w_in[i], dn_conv_w[i], dn_a_log[i],
                             dn_dt_bias[i], dn_norm_g[i], cf_dw_w[i], cf_dw_b[i],
                             cf_ln_g[i], cf_ln_b[i], w_out[i])
        h = h + memory_cross_attention(rms_norm(h, norm_xa_g[i]), rms_norm(mem, norm_mem_g[i]),
                                       xa_wq[i], xa_wk[i], xa_wv[i], xa_wo[i])
        h = h + squared_relu_mlp(rms_norm(h, norm_mlp_g[i]), mlp_w1[i], mlp_w2[i])
    return rms_norm(h, norm_final_g)
```

```python
import functools

import jax
import jax.numpy as jnp
from jax import lax
from jax.experimental import pallas as pl
from jax.experimental.pallas import tpu as pltpu

F32 = jnp.float32
BF16 = jnp.bfloat16
HIGHEST = lax.Precision.HIGHEST

EPS = 1e-6
DN_HEADS = 8
DN_HEAD_DIM = 128
DN_CONV_W = 4
DN_CHUNK = 64
CF_CONV_W = 31
XA_HEADS = 4
LANES = 128
SUBLANES = 8
GATE_COLS = 256
VMEM_LIMIT = 56 * 1024 * 1024


def _params(*sem):
    return pltpu.CompilerParams(dimension_semantics=sem, vmem_limit_bytes=VMEM_LIMIT)


def _rms(x, g):
    return x * lax.rsqrt(jnp.mean(x * x, axis=-1, keepdims=True) + EPS) * g


def _silu(x):
    return x * jax.nn.sigmoid(x)


def _softplus(x):
    return jnp.maximum(x, 0.0) + jnp.log1p(jnp.exp(-jnp.abs(x)))


def _dot(a, b):
    return jnp.dot(a, b, preferred_element_type=F32)


def _hdot(a, b):
    return jnp.dot(a, b, precision=HIGHEST, preferred_element_type=F32)


def _hdot_nt(a, b):
    return lax.dot_general(a, b, (((1,), (1,)), ((), ())), precision=HIGHEST,
                           preferred_element_type=F32)


def _norm_matmul_kernel(x_ref, g_ref, w_ref, o_ref, xn_ref):
    @pl.when(pl.program_id(1) == 0)
    def _():
        xn_ref[...] = _rms(x_ref[...], g_ref[...]).astype(BF16)

    o_ref[...] = _dot(xn_ref[...], w_ref[...]).astype(o_ref.dtype)


def _norm_matmul(x, g, w, tm, tn, out_dtype=F32):
    m, k = x.shape
    n = w.shape[1]
    return pl.pallas_call(
        _norm_matmul_kernel,
        grid=(m // tm, n // tn),
        in_specs=[pl.BlockSpec((tm, k), lambda i, j: (i, 0)),
                  pl.BlockSpec((1, k), lambda i, j: (0, 0)),
                  pl.BlockSpec((k, tn), lambda i, j: (0, j))],
        out_specs=pl.BlockSpec((tm, tn), lambda i, j: (i, j)),
        out_shape=jax.ShapeDtypeStruct((m, n), out_dtype),
        scratch_shapes=[pltpu.VMEM((tm, k), BF16)],
        compiler_params=_params("parallel", "arbitrary"),
        name="norm_matmul",
    )(x, g, w)


def _deltanet_kernel(q_ref, k_ref, v_ref, z_ref, ba_ref, cwq_ref, cwk_ref, cwv_ref,
                     alog_ref, dtb_ref, ng_ref, o_ref,
                     u_s, w_s, qd_s, kd_s, at_s, egl_s, *, seq, group):
    C = DN_CHUNK
    D = DN_HEAD_DIM
    h = pl.program_id(1)
    n_chunks = seq // C

    row = lax.broadcasted_iota(jnp.int32, (C, C), 0)
    col = lax.broadcasted_iota(jnp.int32, (C, C), 1)
    causal = row >= col
    strict = row > col
    tril = causal.astype(F32)
    eye = (row == col).astype(F32)
    lane = lax.broadcasted_iota(jnp.int32, (1, GATE_COLS), 1)

    def conv_silu(x_ref, w_ref, r0):
        x = x_ref[pl.ds(r0, C), :]
        prev = x_ref[pl.ds(jnp.maximum(r0 - SUBLANES, 0), SUBLANES), :]
        prev = jnp.where(r0 > 0, prev, 0.0)
        xw = jnp.concatenate([prev, x], axis=0)
        acc = x * w_ref[DN_CONV_W - 1:DN_CONV_W, :]
        for j in range(DN_CONV_W - 1):
            shifted = pltpu.roll(xw, DN_CONV_W - 1 - j, axis=0)[SUBLANES:, :]
            acc = acc + shifted * w_ref[j:j + 1, :]
        return _silu(acc)

    def local_chunk(n):
        r0 = pl.multiple_of(n * C, C)
        q = conv_silu(q_ref, cwq_ref, r0)
        k = conv_silu(k_ref, cwk_ref, r0)
        v = conv_silu(v_ref, cwv_ref, r0)
        q = q * lax.rsqrt(jnp.sum(q * q, axis=-1, keepdims=True) + EPS) * (D ** -0.5)
        k = k * lax.rsqrt(jnp.sum(k * k, axis=-1, keepdims=True) + EPS)

        ba = ba_ref[pl.ds(r0, C), :]
        beta = jnp.sum(jnp.where(lane == h, jax.nn.sigmoid(ba), 0.0), axis=-1, keepdims=True)
        g_all = -jnp.exp(alog_ref[...]) * _softplus(ba + dtb_ref[...])
        g = jnp.sum(jnp.where(lane == DN_HEADS + h, g_all, 0.0), axis=-1, keepdims=True)

        gc = _hdot(tril, jnp.broadcast_to(g, (C, D)))
        gc_row = gc.T[:C, :]
        diff = jnp.where(causal, gc[:, :C] - gc_row, 0.0)
        decay = jnp.where(causal, jnp.exp(diff), 0.0)
        eg = jnp.exp(gc)

        kb = k * beta
        vb = v * beta
        a = _hdot_nt(kb, k) * decay
        nil = jnp.where(strict, -a, 0.0)
        inv = eye + nil
        pw = nil
        for _ in range(5):
            pw = _hdot(pw, pw)
            inv = inv + _hdot(inv, pw)
        u_s[pl.ds(r0, C), :] = _hdot(inv, vb)
        w_s[pl.ds(r0, C), :] = _hdot(inv, kb * eg)
        at_s[pl.ds(r0, C), :] = _hdot_nt(q, k) * decay
        qd_s[pl.ds(r0, C), :] = q * eg
        g_last = gc[C - 1:C, :]
        kd_s[pl.ds(r0, C), :] = k * jnp.exp(g_last - gc)
        egl_s[n] = jnp.broadcast_to(jnp.exp(g_last), (SUBLANES, D))

    def local_group(i, carry):
        for c in range(group):
            local_chunk(i * group + c)
        return carry

    lax.fori_loop(0, n_chunks // group, local_group, 0)

    def scan_chunk(n, state):
        r0 = pl.multiple_of(n * C, C)
        v_new = u_s[pl.ds(r0, C), :] - _hdot(w_s[pl.ds(r0, C), :], state)
        o = _hdot(qd_s[pl.ds(r0, C), :], state) + _hdot(at_s[pl.ds(r0, C), :], v_new)
        state = state * egl_s[n][0:1, :] + _hdot(kd_s[pl.ds(r0, C), :].T, v_new)
        o = _rms(o, ng_ref[...]) * _silu(z_ref[pl.ds(r0, C), :])
        o_ref[pl.ds(r0, C), :] = o
        return state

    lax.fori_loop(0, n_chunks, scan_chunk, jnp.zeros((D, D), F32))


def _deltanet(proj, dn_conv_w, alog_row, dtb_row, dn_norm_g, batch, seq):
    D = DN_HEAD_DIM
    H = DN_HEADS
    col_blk = lambda off: pl.BlockSpec((seq, D), lambda b, h, off=off: (b, off + h))
    cw_blk = lambda off: pl.BlockSpec((DN_CONV_W, D), lambda b, h, off=off: (0, off + h))
    row_blk = lambda w: pl.BlockSpec((1, w), lambda b, h: (0, 0))
    gate_blk = 6 * 1024 // GATE_COLS
    return pl.pallas_call(
        functools.partial(_deltanet_kernel, seq=seq, group=4),
        grid=(batch, H),
        in_specs=[col_blk(0), col_blk(H), col_blk(2 * H), col_blk(3 * H),
                  pl.BlockSpec((seq, GATE_COLS), lambda b, h: (b, gate_blk)),
                  cw_blk(0), cw_blk(H), cw_blk(2 * H),
                  row_blk(GATE_COLS), row_blk(GATE_COLS), row_blk(D)],
        out_specs=pl.BlockSpec((seq, D), lambda b, h: (b, h)),
        out_shape=jax.ShapeDtypeStruct((batch * seq, H * D), F32),
        scratch_shapes=[pltpu.VMEM((seq, D), F32), pltpu.VMEM((seq, D), F32),
                        pltpu.VMEM((seq, D), F32), pltpu.VMEM((seq, D), F32),
                        pltpu.VMEM((seq, DN_CHUNK), F32),
                        pltpu.VMEM((seq // DN_CHUNK, SUBLANES, D), F32)],
        compiler_params=_params("parallel", "parallel"),
        name="deltanet",
    )(proj, proj, proj, proj, proj, dn_conv_w, dn_conv_w, dn_conv_w,
      alog_row, dtb_row, dn_norm_g)


CF_HALO = 32
CF_ROWS = 64


def _conformer_kernel(a_ref, gt_ref, ah_ref, gh_ref, w_ref, b_ref, lg_ref, lb_ref, o_ref,
                      buf, ybuf, *, tt, ch):
    t = pl.program_id(1)
    halo = ah_ref[...] * jax.nn.sigmoid(gh_ref[...])
    buf[0:CF_HALO, :] = jnp.where(t > 0, halo, 0.0)
    buf[CF_HALO:, :] = a_ref[...] * jax.nn.sigmoid(gt_ref[...])

    win_rows = CF_ROWS + CF_HALO
    lead = CF_HALO - (CF_CONV_W - 1)

    def conv_rows(i, carry):
        r0 = pl.multiple_of(i * CF_ROWS, CF_ROWS)
        for cb in range(ch // LANES):
            cs = slice(cb * LANES, (cb + 1) * LANES)
            win = buf[pl.ds(r0, win_rows), cs]
            acc = buf[pl.ds(r0 + CF_HALO, CF_ROWS), cs] * w_ref[CF_CONV_W - 1:CF_CONV_W, cs]
            for sub in range(SUBLANES):
                wb = win if sub == 0 else pltpu.roll(win, win_rows - sub, axis=0)
                for blk in range(CF_HALO // SUBLANES):
                    j = blk * SUBLANES + sub - lead
                    if 0 <= j < CF_CONV_W - 1:
                        acc = acc + wb[blk * SUBLANES:blk * SUBLANES + CF_ROWS, :] * w_ref[j:j + 1, cs]
            ybuf[pl.ds(r0, CF_ROWS), cs] = acc + b_ref[:, cs]
        return carry

    lax.fori_loop(0, tt // CF_ROWS, conv_rows, 0)

    ln_rows = 32

    def ln_rows_fn(i, carry):
        r0 = pl.multiple_of(i * ln_rows, ln_rows)
        y = ybuf[pl.ds(r0, ln_rows), :]
        mu = jnp.mean(y, axis=-1, keepdims=True)
        yc = y - mu
        var = jnp.mean(yc * yc, axis=-1, keepdims=True)
        yn = yc * lax.rsqrt(var + EPS) * lg_ref[...] + lb_ref[...]
        o_ref[pl.ds(r0, ln_rows), :] = _silu(yn).astype(o_ref.dtype)
        return carry

    lax.fori_loop(0, tt // ln_rows, ln_rows_fn, 0)


def _conformer(proj, cf_dw_w, cf_dw_b, cf_ln_g, cf_ln_b, batch, seq, tt):
    ch = cf_dw_w.shape[1]
    nt = seq // tt
    a_col = 4 * 1024 // ch
    g_col = a_col + 1
    hb = tt // CF_HALO
    main = lambda c: pl.BlockSpec((tt, ch), lambda b, t, c=c: (b * nt + t, c))
    halo = lambda c: pl.BlockSpec(
        (CF_HALO, ch), lambda b, t, c=c: (jnp.maximum((b * nt + t) * hb - 1, 0), c))
    row = lambda r: pl.BlockSpec((r, ch), lambda b, t: (0, 0))
    return pl.pallas_call(
        functools.partial(_conformer_kernel, tt=tt, ch=ch),
        grid=(batch, nt),
        in_specs=[main(a_col), main(g_col), halo(a_col), halo(g_col),
                  row(CF_CONV_W), row(1), row(1), row(1)],
        out_specs=pl.BlockSpec((tt, ch), lambda b, t: (b * nt + t, 0)),
        out_shape=jax.ShapeDtypeStruct((batch * seq, ch), F32),
        scratch_shapes=[pltpu.VMEM((tt + CF_HALO, ch), F32), pltpu.VMEM((tt, ch), F32)],
        compiler_params=_params("parallel", "parallel"),
        name="conformer",
    )(proj, proj, proj, proj, cf_dw_w, cf_dw_b, cf_ln_g, cf_ln_b)


def _out_proj_kernel(x_ref, o_ref, c_ref, wo_ref, wc_ref, h_ref):
    acc = _dot(o_ref[...].astype(BF16), wo_ref[...])
    acc = acc + _dot(c_ref[...].astype(BF16), wc_ref[...])
    h_ref[...] = x_ref[...] + acc


def _out_proj(x, o, c, w_o, w_c, tm):
    m, d = x.shape
    ko, kc = o.shape[1], c.shape[1]
    const = lambda shape: pl.BlockSpec(shape, lambda i: (0, 0), pipeline_mode=pl.Buffered(1))
    return pl.pallas_call(
        _out_proj_kernel,
        grid=(m // tm,),
        in_specs=[pl.BlockSpec((tm, d), lambda i: (i, 0)),
                  pl.BlockSpec((tm, ko), lambda i: (i, 0)),
                  pl.BlockSpec((tm, kc), lambda i: (i, 0)),
                  const((ko, d)), const((kc, d))],
        out_specs=pl.BlockSpec((tm, d), lambda i: (i, 0)),
        out_shape=jax.ShapeDtypeStruct((m, d), F32),
        compiler_params=_params("parallel"),
        name="out_proj",
    )(x, o, c, w_o, w_c)


def _xattn_kernel(h_ref, g_ref, wq_ref, kv_ref, wo_ref, o_ref, *, d_model):
    dh = d_model // XA_HEADS
    h1 = h_ref[...]
    hn = _rms(h1, g_ref[...]).astype(BF16)
    q = _dot(hn, wq_ref[...]).astype(BF16)
    heads = []
    for hd in range(XA_HEADS):
        cs = slice(hd * dh, (hd + 1) * dh)
        k = kv_ref[:, cs]
        v = kv_ref[:, d_model + hd * dh:d_model + (hd + 1) * dh]
        s = lax.dot_general(q[:, cs], k, (((1,), (1,)), ((), ())),
                            preferred_element_type=F32) * (dh ** -0.5)
        p = jnp.exp(s - jnp.max(s, axis=-1, keepdims=True))
        p = p / jnp.sum(p, axis=-1, keepdims=True)
        heads.append(_dot(p.astype(BF16), v).astype(BF16))
    o = jnp.concatenate(heads, axis=-1)
    o_ref[...] = h1 + _dot(o, wo_ref[...])


def _xattn(h1, g, wq, kv, wo, batch, seq, tm):
    m, d = h1.shape
    nt = seq // tm
    n_mem = kv.shape[0] // batch
    const = lambda shape: pl.BlockSpec(shape, lambda b, t: (0, 0), pipeline_mode=pl.Buffered(1))
    return pl.pallas_call(
        functools.partial(_xattn_kernel, d_model=d),
        grid=(batch, nt),
        in_specs=[pl.BlockSpec((tm, d), lambda b, t: (b * nt + t, 0)),
                  const((1, d)), const((d, d)),
                  pl.BlockSpec((n_mem, 2 * d), lambda b, t: (b, 0)),
                  const((d, d))],
        out_specs=pl.BlockSpec((tm, d), lambda b, t: (b * nt + t, 0)),
        out_shape=jax.ShapeDtypeStruct((m, d), F32),
        compiler_params=_params("parallel", "parallel"),
        name="xattn",
    )(h1, g, wq, kv, wo)


def _mlp_kernel(h_ref, g_ref, w1_ref, w2_ref, gf_ref, o_ref, hn_s, acc_s):
    j = pl.program_id(1)

    @pl.when(j == 0)
    def _():
        hn_s[...] = _rms(h_ref[...], g_ref[...]).astype(BF16)
        acc_s[...] = jnp.zeros_like(acc_s)

    hid = jnp.maximum(_dot(hn_s[...], w1_ref[...]), 0.0)
    acc_s[...] += _dot((hid * hid).astype(BF16), w2_ref[...])

    @pl.when(j == pl.num_programs(1) - 1)
    def _():
        o_ref[...] = _rms(h_ref[...] + acc_s[...], gf_ref[...])


def _mlp(h2, g, w1, w2, gf, tm, tf):
    m, d = h2.shape
    ff = w1.shape[1]
    return pl.pallas_call(
        _mlp_kernel,
        grid=(m // tm, ff // tf),
        in_specs=[pl.BlockSpec((tm, d), lambda i, j: (i, 0)),
                  pl.BlockSpec((1, d), lambda i, j: (0, 0)),
                  pl.BlockSpec((d, tf), lambda i, j: (0, j)),
                  pl.BlockSpec((tf, d), lambda i, j: (j, 0)),
                  pl.BlockSpec((1, d), lambda i, j: (0, 0))],
        out_specs=pl.BlockSpec((tm, d), lambda i, j: (i, 0)),
        out_shape=jax.ShapeDtypeStruct((m, d), F32),
        scratch_shapes=[pltpu.VMEM((tm, d), BF16), pltpu.VMEM((tm, d), F32)],
        compiler_params=_params("parallel", "arbitrary"),
        name="mlp",
    )(h2, g, w1, w2, gf)


def kernel(x, mem, norm_mix_g, w_in, dn_conv_w, dn_a_log, dn_dt_bias, dn_norm_g, cf_dw_w, cf_dw_b, cf_ln_g, cf_ln_b, w_out, norm_xa_g, norm_mem_g, xa_wq, xa_wk, xa_wv, xa_wo, norm_mlp_g, mlp_w1, mlp_w2, norm_final_g):
    batch, seq, d = x.shape
    n_mem = mem.shape[1]
    d_delta = DN_HEADS * DN_HEAD_DIM
    assert w_in.shape[0] == 1, "single-layer block"
    row = lambda v: v.reshape(1, -1).astype(F32)

    w_in0 = w_in[0]
    n_gate = 2 * DN_HEADS
    w_proj = jnp.concatenate(
        [w_in0[:, :4 * d_delta], w_in0[:, 4 * d_delta + n_gate:], w_in0[:, 4 * d_delta:4 * d_delta + n_gate],
         jnp.zeros((d, GATE_COLS - n_gate), F32)], axis=1).astype(BF16)
    pad_gate = lambda v: jnp.pad(v.reshape(1, -1).astype(F32), ((0, 0), (DN_HEADS, GATE_COLS - n_gate)))

    x2 = x.reshape(batch * seq, d)
    proj = _norm_matmul(x2, row(norm_mix_g[0]), w_proj, tm=512, tn=1280)

    o = _deltanet(proj, dn_conv_w[0], pad_gate(dn_a_log[0]), pad_gate(dn_dt_bias[0]),
                  row(dn_norm_g[0]), batch, seq)
    c = _conformer(proj, cf_dw_w[0], row(cf_dw_b[0]), row(cf_ln_g[0]), row(cf_ln_b[0]),
                   batch, seq, tt=512)

    w_out0 = w_out[0].astype(BF16)
    h1 = _out_proj(x2, o, c, w_out0[:d_delta], w_out0[d_delta:], tm=512)

    w_kv = jnp.concatenate([xa_wk[0], xa_wv[0]], axis=1).astype(BF16)
    kv = _norm_matmul(mem.reshape(batch * n_mem, d), row(norm_mem_g[0]), w_kv,
                      tm=256, tn=1024, out_dtype=BF16)
    h2 = _xattn(h1, row(norm_xa_g[0]), xa_wq[0].astype(BF16), kv, xa_wo[0].astype(BF16),
                batch, seq, tm=512)

    out = _mlp(h2, row(norm_mlp_g[0]), mlp_w1[0].astype(BF16), mlp_w2[0].astype(BF16),
               row(norm_final_g), tm=512, tf=1024)
    return out.reshape(batch, seq, d)
```

```python
import functools

import jax
import jax.numpy as jnp
from jax import lax
from jax.experimental import pallas as pl
from jax.experimental.pallas import tpu as pltpu

F32 = jnp.float32
BF16 = jnp.bfloat16

EPS = 1e-6
DN_HEADS = 8
DN_HEAD_DIM = 128
DN_CONV_W = 4
DN_CHUNK = 64
CF_CONV_W = 31
XA_HEADS = 4
LANES = 128
SUBLANES = 8
VMEM_LIMIT = 56 * 1024 * 1024


def _params(*sem):
    return pltpu.CompilerParams(dimension_semantics=sem, vmem_limit_bytes=VMEM_LIMIT)


def _rms(x, g):
    return x * lax.rsqrt(jnp.mean(x * x, axis=-1, keepdims=True) + EPS) * g


def _silu(x):
    return x * jax.nn.sigmoid(x)


def _softplus(x):
    return jnp.maximum(x, 0.0) + jnp.log1p(jnp.exp(-jnp.abs(x)))


def _dot(a, b):
    return jnp.dot(a, b, preferred_element_type=F32)


def _split(a):
    hi = a.astype(BF16)
    return hi, (a - hi.astype(F32)).astype(BF16)


def _norm_matmul_kernel(x_ref, g_ref, w_ref, o_ref, xn_ref):
    @pl.when(pl.program_id(1) == 0)
    def _():
        xn_ref[...] = _rms(x_ref[...], g_ref[...]).astype(BF16)

    o_ref[...] = _dot(xn_ref[...], w_ref[...]).astype(o_ref.dtype)


def _norm_matmul(x, g, w, tm, tn, out_dtype=F32):
    m, k = x.shape
    n = w.shape[1]
    return pl.pallas_call(
        _norm_matmul_kernel,
        grid=(m // tm, n // tn),
        in_specs=[pl.BlockSpec((tm, k), lambda i, j: (i, 0)),
                  pl.BlockSpec((1, k), lambda i, j: (0, 0)),
                  pl.BlockSpec((k, tn), lambda i, j: (0, j))],
        out_specs=pl.BlockSpec((tm, tn), lambda i, j: (i, j)),
        out_shape=jax.ShapeDtypeStruct((m, n), out_dtype),
        scratch_shapes=[pltpu.VMEM((tm, k), BF16)],
        compiler_params=_params("parallel", "arbitrary"),
        name="norm_matmul",
    )(x, g, w)


DN_GROUP = 4
DN_WIDE = 4
DN_UNROLL = 4
DN_TILE = 2048


def _dn_kernel(q_ref, k_ref, v_ref, z_ref, gt_ref, cwq_ref, cwk_ref, cwv_ref,
               alog_ref, dtb_ref, ng_ref, o_ref,
               state_s, prev_s, u_s, wq_s, ak_s, egl_s, *, tile, hg, gw, unroll):
    C = DN_CHUNK
    D = DN_HEAD_DIM
    t = pl.program_id(2)
    nc = tile // C
    WD = gw * D
    WC = gw * C
    n_wide = hg // gw

    @pl.when(t == 0)
    def _():
        state_s[...] = jnp.zeros_like(state_s)
        prev_s[...] = jnp.zeros_like(prev_s)

    row_l = lax.broadcasted_iota(jnp.int32, (C, LANES), 0)
    first_half = lax.broadcasted_iota(jnp.int32, (C, LANES), 1) < C
    row_c = lax.broadcasted_iota(jnp.int32, (C, WC), 0)
    col_c = lax.broadcasted_iota(jnp.int32, (C, WC), 1) % C
    causal = row_c >= col_c
    strict = row_c > col_c
    eye = (row_c == col_c).astype(F32)

    def block_diag(x, rows_per_head, cols_per_head):
        tall = jnp.concatenate([x] * gw, axis=0)
        rb = lax.broadcasted_iota(jnp.int32, tall.shape, 0) // rows_per_head
        cb = (lax.broadcasted_iota(jnp.int32, tall.shape, 1) // cols_per_head) % gw
        return jnp.where(rb == cb, tall, jnp.zeros_like(tall))

    def head_cols(mat, lane0):
        return jnp.concatenate(
            [jnp.broadcast_to(mat[:, lane0 + i:lane0 + i + 1], (mat.shape[0], D)) for i in range(gw)],
            axis=1)

    def head_l2(x, scale):
        parts = []
        for i in range(gw):
            xi = x[:, i * D:(i + 1) * D]
            parts.append(xi * (lax.rsqrt(jnp.sum(xi * xi, axis=-1, keepdims=True) + EPS) * scale))
        return jnp.concatenate(parts, axis=1)

    def conv_silu(x_ref, w_ref, which, r0, cs, first):
        x = x_ref[pl.ds(r0, C), cs]
        if first:
            prev = prev_s[which, :, cs]
        else:
            prev = x_ref[pl.ds(r0 - SUBLANES, SUBLANES), cs]
        xw = jnp.concatenate([prev, x], axis=0)
        acc = x * w_ref[DN_CONV_W - 1:DN_CONV_W, cs]
        for j in range(DN_CONV_W - 1):
            shifted = pltpu.roll(xw, DN_CONV_W - 1 - j, axis=0)[SUBLANES:, :]
            acc = acc + shifted * w_ref[j:j + 1, cs]
        return _silu(acc)

    def dot3_wide(lhs, rhs):
        lh, ll = _split(lhs)
        rh, rl = _split(rhs)
        bh = block_diag(rh, C, C)
        bl = block_diag(rl, C, C)
        return _dot(jnp.concatenate([lh, lh, ll], axis=1), jnp.concatenate([bh, bl, bh], axis=0))

    def local(n, par, c, first=False):
        r0 = n * C if isinstance(n, int) else pl.multiple_of(n * C, C)
        gt = gt_ref[pl.ds(r0, C), :]
        beta_all = jax.nn.sigmoid(gt)
        gc_all = -jnp.exp(alog_ref[...]) * _softplus(gt + dtb_ref[...])
        for s in (1, 2, 4, 8, 16, 32):
            gc_all = gc_all + jnp.where(row_l >= s, pltpu.roll(gc_all, s, axis=0), 0.0)
        g_last_all = gc_all[C - 1:C, :]
        eg_all = jnp.exp(gc_all)
        ekd_all = jnp.exp(g_last_all - gc_all)
        egl_all = jnp.broadcast_to(jnp.exp(g_last_all), (SUBLANES, LANES))
        pair_t = jnp.concatenate([gc_all, pltpu.roll(gc_all, LANES - 1, axis=1)], axis=0).T

        for wi in range(n_wide):
            h0 = wi * gw
            cs = slice(h0 * D, (h0 + gw) * D)
            la = hg + h0
            g_col = jnp.concatenate(
                [jnp.where(first_half,
                           jnp.broadcast_to(gc_all[:, la + 2 * p:la + 2 * p + 1], (C, LANES)),
                           jnp.broadcast_to(gc_all[:, la + 2 * p + 1:la + 2 * p + 2], (C, LANES)))
                 for p in range(gw // 2)], axis=1)
            g_row = jnp.concatenate(
                [pair_t[la + 2 * p:la + 2 * p + 1, :] for p in range(gw // 2)], axis=1)
            decay = jnp.where(causal, jnp.exp(jnp.where(causal, g_col - g_row, 0.0)), 0.0)

            q = head_l2(conv_silu(q_ref, cwq_ref, 0, r0, cs, first), D ** -0.5)
            k = head_l2(conv_silu(k_ref, cwk_ref, 1, r0, cs, first), 1.0)
            v = conv_silu(v_ref, cwv_ref, 2, r0, cs, first)
            beta = head_cols(beta_all, h0)
            eg = head_cols(eg_all, la)
            kb = k * beta
            raw = lax.dot_general(jnp.concatenate([kb, q], axis=0).astype(BF16),
                                  block_diag(k.astype(BF16), C, D),
                                  (((1,), (1,)), ((), ())), preferred_element_type=F32)
            nil = jnp.where(strict, -(raw[:C] * decay), 0.0)
            yield
            inv = eye + nil
            pw = dot3_wide(nil, nil)
            yield
            for _ in range(4):
                both = dot3_wide(jnp.concatenate([inv, pw], axis=0), pw)
                inv = inv + both[:C]
                pw = both[C:]
                yield
            inv = inv + dot3_wide(inv, pw)
            yield
            rhs = jnp.concatenate([v * beta, kb * eg], axis=1).astype(BF16)
            uw = _dot(inv.astype(BF16), block_diag(rhs, C, D))
            u_s[par, c, :, cs] = uw[:, :WD]
            wq_s[par, c, :, cs] = jnp.concatenate([uw[:, WD:], q * eg], axis=0).astype(BF16)
            kd = k * head_cols(ekd_all, la)
            kd_t = jnp.concatenate(
                [jnp.concatenate([kd[:, (2 * p) * D:(2 * p + 1) * D],
                                  kd[:, (2 * p + 1) * D:(2 * p + 2) * D]], axis=0).T
                 for p in range(gw // 2)], axis=1)
            ak_s[par, c, :, h0 * C:(h0 + gw) * C] = jnp.concatenate(
                [raw[C:] * decay, kd_t], axis=0).astype(BF16)
            egl_s[par, c, :, cs] = head_cols(egl_all, la)
            yield

    def scans(first_chunk, par):
        for c in range(unroll):
            n = first_chunk + c
            r0 = n * C if isinstance(n, int) else pl.multiple_of(n * C, C)
            for wi in range(n_wide):
                h0 = wi * gw
                cs = slice(h0 * D, (h0 + gw) * D)
                state = state_s[:, cs]
                r1 = _dot(wq_s[par, c, :, cs], block_diag(state.astype(BF16), D, D))
                v_new = u_s[par, c, :, cs] - r1[:C]
                yield
                r2 = _dot(ak_s[par, c, :, h0 * C:(h0 + gw) * C],
                          block_diag(v_new.astype(BF16), C, D))
                state_s[:, cs] = state * egl_s[par, c, 0:1, cs] + r2[C:]
                o = r1[C:] + r2[:C]
                gate = _silu(z_ref[pl.ds(r0, C), cs])
                for i in range(gw):
                    hs = slice(i * D, (i + 1) * D)
                    o_ref[pl.ds(r0, C), (h0 + i) * D:(h0 + i + 1) * D] = (
                        _rms(o[:, hs], ng_ref[...]) * gate[:, hs])
                yield

    def run_interleaved(gens):
        live = list(gens)
        while live:
            nxt = []
            for g in live:
                try:
                    next(g)
                    nxt.append(g)
                except StopIteration:
                    pass
            live = nxt

    n_groups = nc // unroll
    run_interleaved([local(c, 0, c, first=(c == 0)) for c in range(unroll)])

    def body(m, carry):
        nxt = (m + 1) * unroll
        run_interleaved([scans(m * unroll, m % 2)]
                        + [local(nxt + c, (m + 1) % 2, c) for c in range(unroll)])
        return carry

    lax.fori_loop(0, n_groups - 1, body, 0)
    run_interleaved([scans((n_groups - 1) * unroll, (n_groups - 1) % 2)])

    prev_s[0] = q_ref[tile - SUBLANES:tile, :]
    prev_s[1] = k_ref[tile - SUBLANES:tile, :]
    prev_s[2] = v_ref[tile - SUBLANES:tile, :]


def _deltanet(proj, dn_conv_w, alog_row, dtb_row, dn_norm_g, batch, seq):
    D = DN_HEAD_DIM
    H = DN_HEADS
    hg = DN_GROUP
    ng = H // hg
    tile = DN_TILE
    nt = seq // tile
    wid = hg * D
    slots = DN_UNROLL
    col_blk = lambda off: pl.BlockSpec((tile, wid), lambda b, g, t, off=off: (b * nt + t, off + g))
    cw_blk = lambda off: pl.BlockSpec((DN_CONV_W, wid), lambda b, g, t, off=off: (0, off + g))
    gate_blk = 6 * 1024 // LANES
    return pl.pallas_call(
        functools.partial(_dn_kernel, tile=tile, hg=hg, gw=DN_WIDE, unroll=DN_UNROLL),
        grid=(batch, ng, nt),
        in_specs=[col_blk(0), col_blk(ng), col_blk(2 * ng), col_blk(3 * ng),
                  pl.BlockSpec((tile, LANES), lambda b, g, t: (b * nt + t, gate_blk + g)),
                  cw_blk(0), cw_blk(ng), cw_blk(2 * ng),
                  pl.BlockSpec((1, LANES), lambda b, g, t: (0, g)),
                  pl.BlockSpec((1, LANES), lambda b, g, t: (0, g)),
                  pl.BlockSpec((1, D), lambda b, g, t: (0, 0))],
        out_specs=pl.BlockSpec((tile, wid), lambda b, g, t: (b * nt + t, g)),
        out_shape=jax.ShapeDtypeStruct((batch * seq, H * D), F32),
        scratch_shapes=[pltpu.VMEM((D, wid), F32),
                        pltpu.VMEM((3, SUBLANES, wid), F32),
                        pltpu.VMEM((2, slots, DN_CHUNK, wid), F32),
                        pltpu.VMEM((2, slots, 2 * DN_CHUNK, wid), BF16),
                        pltpu.VMEM((2, slots, DN_CHUNK + D, hg * DN_CHUNK), BF16),
                        pltpu.VMEM((2, slots, SUBLANES, wid), F32)],
        compiler_params=_params("parallel", "parallel", "arbitrary"),
        name="deltanet",
    )(proj, proj, proj, proj, proj, dn_conv_w, dn_conv_w, dn_conv_w,
      alog_row, dtb_row, dn_norm_g)


CF_HALO = 32
CF_ROWS = 64


def _conformer_kernel(a_ref, gt_ref, ah_ref, gh_ref, w_ref, b_ref, lg_ref, lb_ref, o_ref,
                      buf, ybuf, *, tt, ch):
    t = pl.program_id(1)
    halo = ah_ref[...] * jax.nn.sigmoid(gh_ref[...])
    buf[0:CF_HALO, :] = jnp.where(t > 0, halo, 0.0)
    buf[CF_HALO:, :] = a_ref[...] * jax.nn.sigmoid(gt_ref[...])

    win_rows = CF_ROWS + CF_HALO
    lead = CF_HALO - (CF_CONV_W - 1)

    def conv_rows(i, carry):
        r0 = pl.multiple_of(i * CF_ROWS, CF_ROWS)
        for cb in range(ch // LANES):
            cs = slice(cb * LANES, (cb + 1) * LANES)
            win = buf[pl.ds(r0, win_rows), cs]
            acc = buf[pl.ds(r0 + CF_HALO, CF_ROWS), cs] * w_ref[CF_CONV_W - 1:CF_CONV_W, cs]
            for sub in range(SUBLANES):
                wb = win if sub == 0 else pltpu.roll(win, win_rows - sub, axis=0)
                for blk in range(CF_HALO // SUBLANES):
                    j = blk * SUBLANES + sub - lead
                    if 0 <= j < CF_CONV_W - 1:
                        acc = acc + wb[blk * SUBLANES:blk * SUBLANES + CF_ROWS, :] * w_ref[j:j + 1, cs]
            ybuf[pl.ds(r0, CF_ROWS), cs] = acc + b_ref[:, cs]
        return carry

    lax.fori_loop(0, tt // CF_ROWS, conv_rows, 0)

    ln_rows = 32

    def ln_rows_fn(i, carry):
        r0 = pl.multiple_of(i * ln_rows, ln_rows)
        y = ybuf[pl.ds(r0, ln_rows), :]
        mu = jnp.mean(y, axis=-1, keepdims=True)
        yc = y - mu
        var = jnp.mean(yc * yc, axis=-1, keepdims=True)
        yn = yc * lax.rsqrt(var + EPS) * lg_ref[...] + lb_ref[...]
        o_ref[pl.ds(r0, ln_rows), :] = _silu(yn).astype(o_ref.dtype)
        return carry

    lax.fori_loop(0, tt // ln_rows, ln_rows_fn, 0)


def _conformer(proj, cf_dw_w, cf_dw_b, cf_ln_g, cf_ln_b, batch, seq, tt):
    ch = cf_dw_w.shape[1]
    nt = seq // tt
    a_col = 4 * 1024 // ch
    g_col = a_col + 1
    hb = tt // CF_HALO
    main = lambda c: pl.BlockSpec((tt, ch), lambda b, t, c=c: (b * nt + t, c))
    halo = lambda c: pl.BlockSpec(
        (CF_HALO, ch), lambda b, t, c=c: (jnp.maximum((b * nt + t) * hb - 1, 0), c))
    row = lambda r: pl.BlockSpec((r, ch), lambda b, t: (0, 0))
    return pl.pallas_call(
        functools.partial(_conformer_kernel, tt=tt, ch=ch),
        grid=(batch, nt),
        in_specs=[main(a_col), main(g_col), halo(a_col), halo(g_col),
                  row(CF_CONV_W), row(1), row(1), row(1)],
        out_specs=pl.BlockSpec((tt, ch), lambda b, t: (b * nt + t, 0)),
        out_shape=jax.ShapeDtypeStruct((batch * seq, ch), F32),
        scratch_shapes=[pltpu.VMEM((tt + CF_HALO, ch), F32), pltpu.VMEM((tt, ch), F32)],
        compiler_params=_params("parallel", "parallel"),
        name="conformer",
    )(proj, proj, proj, proj, cf_dw_w, cf_dw_b, cf_ln_g, cf_ln_b)


def _out_proj_kernel(x_ref, o_ref, c_ref, wo_ref, wc_ref, h_ref):
    acc = _dot(o_ref[...].astype(BF16), wo_ref[...])
    acc = acc + _dot(c_ref[...].astype(BF16), wc_ref[...])
    h_ref[...] = x_ref[...] + acc


def _out_proj(x, o, c, w_o, w_c, tm):
    m, d = x.shape
    ko, kc = o.shape[1], c.shape[1]
    const = lambda shape: pl.BlockSpec(shape, lambda i: (0, 0), pipeline_mode=pl.Buffered(1))
    return pl.pallas_call(
        _out_proj_kernel,
        grid=(m // tm,),
        in_specs=[pl.BlockSpec((tm, d), lambda i: (i, 0)),
                  pl.BlockSpec((tm, ko), lambda i: (i, 0)),
                  pl.BlockSpec((tm, kc), lambda i: (i, 0)),
                  const((ko, d)), const((kc, d))],
        out_specs=pl.BlockSpec((tm, d), lambda i: (i, 0)),
        out_shape=jax.ShapeDtypeStruct((m, d), F32),
        compiler_params=_params("parallel"),
        name="out_proj",
    )(x, o, c, w_o, w_c)


def _xattn_kernel(h_ref, g_ref, wq_ref, kv_ref, wo_ref, o_ref, *, d_model):
    dh = d_model // XA_HEADS
    h1 = h_ref[...]
    hn = _rms(h1, g_ref[...]).astype(BF16)
    q = _dot(hn, wq_ref[...]).astype(BF16)
    heads = []
    for hd in range(XA_HEADS):
        cs = slice(hd * dh, (hd + 1) * dh)
        k = kv_ref[:, cs]
        v = kv_ref[:, d_model + hd * dh:d_model + (hd + 1) * dh]
        s = lax.dot_general(q[:, cs], k, (((1,), (1,)), ((), ())),
                            preferred_element_type=F32) * (dh ** -0.5)
        p = jnp.exp(s - jnp.max(s, axis=-1, keepdims=True))
        p = p / jnp.sum(p, axis=-1, keepdims=True)
        heads.append(_dot(p.astype(BF16), v).astype(BF16))
    o = jnp.concatenate(heads, axis=-1)
    o_ref[...] = h1 + _dot(o, wo_ref[...])


def _xattn(h1, g, wq, kv, wo, batch, seq, tm):
    m, d = h1.shape
    nt = seq // tm
    n_mem = kv.shape[0] // batch
    const = lambda shape: pl.BlockSpec(shape, lambda b, t: (0, 0), pipeline_mode=pl.Buffered(1))
    return pl.pallas_call(
        functools.partial(_xattn_kernel, d_model=d),
        grid=(batch, nt),
        in_specs=[pl.BlockSpec((tm, d), lambda b, t: (b * nt + t, 0)),
                  const((1, d)), const((d, d)),
                  pl.BlockSpec((n_mem, 2 * d), lambda b, t: (b, 0)),
                  const((d, d))],
        out_specs=pl.BlockSpec((tm, d), lambda b, t: (b * nt + t, 0)),
        out_shape=jax.ShapeDtypeStruct((m, d), F32),
        compiler_params=_params("parallel", "parallel"),
        name="xattn",
    )(h1, g, wq, kv, wo)


def _mlp_kernel(h_ref, g_ref, w1_ref, w2_ref, gf_ref, o_ref, hn_s, acc_s):
    j = pl.program_id(1)

    @pl.when(j == 0)
    def _():
        hn_s[...] = _rms(h_ref[...], g_ref[...]).astype(BF16)
        acc_s[...] = jnp.zeros_like(acc_s)

    hid = jnp.maximum(_dot(hn_s[...], w1_ref[...]), 0.0)
    acc_s[...] += _dot((hid * hid).astype(BF16), w2_ref[...])

    @pl.when(j == pl.num_programs(1) - 1)
    def _():
        o_ref[...] = _rms(h_ref[...] + acc_s[...], gf_ref[...])


def _mlp(h2, g, w1, w2, gf, tm, tf):
    m, d = h2.shape
    ff = w1.shape[1]
    return pl.pallas_call(
        _mlp_kernel,
        grid=(m // tm, ff // tf),
        in_specs=[pl.BlockSpec((tm, d), lambda i, j: (i, 0)),
                  pl.BlockSpec((1, d), lambda i, j: (0, 0)),
                  pl.BlockSpec((d, tf), lambda i, j: (0, j)),
                  pl.BlockSpec((tf, d), lambda i, j: (j, 0)),
                  pl.BlockSpec((1, d), lambda i, j: (0, 0))],
        out_specs=pl.BlockSpec((tm, d), lambda i, j: (i, 0)),
        out_shape=jax.ShapeDtypeStruct((m, d), F32),
        scratch_shapes=[pltpu.VMEM((tm, d), BF16), pltpu.VMEM((tm, d), F32)],
        compiler_params=_params("parallel", "arbitrary"),
        name="mlp",
    )(h2, g, w1, w2, gf)


def kernel(x, mem, norm_mix_g, w_in, dn_conv_w, dn_a_log, dn_dt_bias, dn_norm_g, cf_dw_w, cf_dw_b, cf_ln_g, cf_ln_b, w_out, norm_xa_g, norm_mem_g, xa_wq, xa_wk, xa_wv, xa_wo, norm_mlp_g, mlp_w1, mlp_w2, norm_final_g):
    batch, seq, d = x.shape
    n_mem = mem.shape[1]
    d_delta = DN_HEADS * DN_HEAD_DIM
    assert w_in.shape[0] == 1, "single-layer block"
    row = lambda v: v.reshape(1, -1).astype(F32)

    w_in0 = w_in[0]
    n_gate = 2 * DN_HEADS
    hg = DN_GROUP
    n_grp = DN_HEADS // hg
    w_beta = w_in0[:, 4 * d_delta:4 * d_delta + DN_HEADS]
    w_alpha = w_in0[:, 4 * d_delta + DN_HEADS:4 * d_delta + n_gate]
    gate_cols = []
    for grp in range(n_grp):
        hs = slice(grp * hg, (grp + 1) * hg)
        gate_cols += [w_beta[:, hs], w_alpha[:, hs], jnp.zeros((d, LANES - 2 * hg), F32)]
    w_proj = jnp.concatenate(
        [w_in0[:, :4 * d_delta], w_in0[:, 4 * d_delta + n_gate:]] + gate_cols, axis=1).astype(BF16)

    def pad_gate(vec):
        v2 = vec.astype(F32).reshape(n_grp, hg)
        return jnp.pad(v2, ((0, 0), (hg, LANES - 2 * hg))).reshape(1, n_grp * LANES)

    x2 = x.reshape(batch * seq, d)
    proj = _norm_matmul(x2, row(norm_mix_g[0]), w_proj, tm=512, tn=1280)

    o = _deltanet(proj, dn_conv_w[0], pad_gate(dn_a_log[0]), pad_gate(dn_dt_bias[0]),
                  row(dn_norm_g[0]), batch, seq)
    c = _conformer(proj, cf_dw_w[0], row(cf_dw_b[0]), row(cf_ln_g[0]), row(cf_ln_b[0]),
                   batch, seq, tt=512)

    w_out0 = w_out[0].astype(BF16)
    h1 = _out_proj(x2, o, c, w_out0[:d_delta], w_out0[d_delta:], tm=512)

    w_kv = jnp.concatenate([xa_wk[0], xa_wv[0]], axis=1).astype(BF16)
    kv = _norm_matmul(mem.reshape(batch * n_mem, d), row(norm_mem_g[0]), w_kv,
                      tm=256, tn=1024, out_dtype=BF16)
    h2 = _xattn(h1, row(norm_xa_g[0]), xa_wq[0].astype(BF16), kv, xa_wo[0].astype(BF16),
                batch, seq, tm=512)

    out = _mlp(h2, row(norm_mlp_g[0]), mlp_w1[0].astype(BF16), mlp_w2[0].astype(BF16),
               row(norm_final_g), tm=512, tf=1024)
    return out.reshape(batch, seq, d)
```

```python
import functools

import jax
import jax.numpy as jnp
from jax import lax
from jax.experimental import pallas as pl
from jax.experimental.pallas import tpu as pltpu

F32 = jnp.float32
BF16 = jnp.bfloat16

EPS = 1e-6
DN_HEADS = 8
DN_HEAD_DIM = 128
DN_CONV_W = 4
DN_CHUNK = 64
CF_CONV_W = 31
XA_HEADS = 4
LANES = 128
SUBLANES = 8
VMEM_LIMIT = 56 * 1024 * 1024


def _params(*sem):
    return pltpu.CompilerParams(dimension_semantics=sem, vmem_limit_bytes=VMEM_LIMIT)


def _rms(x, g):
    return x * lax.rsqrt(jnp.mean(x * x, axis=-1, keepdims=True) + EPS) * g


def _silu(x):
    return x * jax.nn.sigmoid(x)


def _softplus(x):
    return jnp.maximum(x, 0.0) + jnp.log1p(jnp.exp(-jnp.abs(x)))


def _dot(a, b):
    return jnp.dot(a, b, preferred_element_type=F32)


def _run_interleaved(gens):
    live = list(gens)
    while live:
        nxt = []
        for g in live:
            try:
                next(g)
                nxt.append(g)
            except StopIteration:
                pass
        live = nxt


def _split(a):
    hi = a.astype(BF16)
    return hi, (a - hi.astype(F32)).astype(BF16)


def _norm_matmul_kernel(x_ref, g_ref, w_ref, o_ref, xn_ref):
    @pl.when(pl.program_id(1) == 0)
    def _():
        xn_ref[...] = _rms(x_ref[...], g_ref[...]).astype(BF16)

    o_ref[...] = _dot(xn_ref[...], w_ref[...]).astype(o_ref.dtype)


def _norm_matmul(x, g, w, tm, tn, out_dtype=F32):
    m, k = x.shape
    n = w.shape[1]
    return pl.pallas_call(
        _norm_matmul_kernel,
        grid=(m // tm, n // tn),
        in_specs=[pl.BlockSpec((tm, k), lambda i, j: (i, 0)),
                  pl.BlockSpec((1, k), lambda i, j: (0, 0)),
                  pl.BlockSpec((k, tn), lambda i, j: (0, j))],
        out_specs=pl.BlockSpec((tm, tn), lambda i, j: (i, j)),
        out_shape=jax.ShapeDtypeStruct((m, n), out_dtype),
        scratch_shapes=[pltpu.VMEM((tm, k), BF16)],
        compiler_params=_params("parallel", "arbitrary"),
        name="norm_matmul",
    )(x, g, w)


DN_GROUP = 4
DN_WIDE = 4
DN_UNROLL = 4
DN_TILE = 2048


def _dn_kernel(q_ref, k_ref, v_ref, z_ref, gt_ref, cwq_ref, cwk_ref, cwv_ref,
               alog_ref, dtb_ref, ng_ref, o_ref,
               state_s, prev_s, u_s, wq_s, ak_s, egl_s, *, tile, hg, gw, unroll):
    C = DN_CHUNK
    D = DN_HEAD_DIM
    t = pl.program_id(2)
    nc = tile // C
    WD = gw * D
    WC = gw * C
    n_wide = hg // gw

    @pl.when(t == 0)
    def _():
        state_s[...] = jnp.zeros_like(state_s)
        prev_s[...] = jnp.zeros_like(prev_s)

    row_l = lax.broadcasted_iota(jnp.int32, (C, LANES), 0)
    first_half = lax.broadcasted_iota(jnp.int32, (C, LANES), 1) < C
    row_c = lax.broadcasted_iota(jnp.int32, (C, WC), 0)
    col_c = lax.broadcasted_iota(jnp.int32, (C, WC), 1) % C
    causal = row_c >= col_c
    strict = row_c > col_c
    eye = (row_c == col_c).astype(F32)

    def block_diag(x, rows_per_head, cols_per_head):
        tall = jnp.concatenate([x] * gw, axis=0)
        rb = lax.broadcasted_iota(jnp.int32, tall.shape, 0) // rows_per_head
        cb = (lax.broadcasted_iota(jnp.int32, tall.shape, 1) // cols_per_head) % gw
        return jnp.where(rb == cb, tall, jnp.zeros_like(tall))

    def head_cols(mat, lane0):
        return jnp.concatenate(
            [jnp.broadcast_to(mat[:, lane0 + i:lane0 + i + 1], (mat.shape[0], D)) for i in range(gw)],
            axis=1)

    def head_l2(x, scale):
        parts = []
        for i in range(gw):
            xi = x[:, i * D:(i + 1) * D]
            parts.append(xi * (lax.rsqrt(jnp.sum(xi * xi, axis=-1, keepdims=True) + EPS) * scale))
        return jnp.concatenate(parts, axis=1)

    def conv_silu(x_ref, w_ref, which, r0, cs, first):
        x = x_ref[pl.ds(r0, C), cs]
        if first:
            prev = prev_s[which, :, cs]
        else:
            prev = x_ref[pl.ds(r0 - SUBLANES, SUBLANES), cs]
        xw = jnp.concatenate([prev, x], axis=0)
        acc = x * w_ref[DN_CONV_W - 1:DN_CONV_W, cs]
        for j in range(DN_CONV_W - 1):
            shifted = pltpu.roll(xw, DN_CONV_W - 1 - j, axis=0)[SUBLANES:, :]
            acc = acc + shifted * w_ref[j:j + 1, cs]
        return _silu(acc)

    def dot3_wide(lhs, rhs):
        lh, ll = _split(lhs)
        rh, rl = _split(rhs)
        bh = block_diag(rh, C, C)
        bl = block_diag(rl, C, C)
        return _dot(jnp.concatenate([lh, lh, ll], axis=1), jnp.concatenate([bh, bl, bh], axis=0))

    def local(n, par, c, first=False):
        r0 = n * C if isinstance(n, int) else pl.multiple_of(n * C, C)
        gt = gt_ref[pl.ds(r0, C), :]
        beta_all = jax.nn.sigmoid(gt)
        gc_all = -jnp.exp(alog_ref[...]) * _softplus(gt + dtb_ref[...])
        for s in (1, 2, 4, 8, 16, 32):
            gc_all = gc_all + jnp.where(row_l >= s, pltpu.roll(gc_all, s, axis=0), 0.0)
        g_last_all = gc_all[C - 1:C, :]
        eg_all = jnp.exp(gc_all)
        ekd_all = jnp.exp(g_last_all - gc_all)
        egl_all = jnp.broadcast_to(jnp.exp(g_last_all), (SUBLANES, LANES))
        pair_t = jnp.concatenate([gc_all, pltpu.roll(gc_all, LANES - 1, axis=1)], axis=0).T

        for wi in range(n_wide):
            h0 = wi * gw
            cs = slice(h0 * D, (h0 + gw) * D)
            la = hg + h0
            g_col = jnp.concatenate(
                [jnp.where(first_half,
                           jnp.broadcast_to(gc_all[:, la + 2 * p:la + 2 * p + 1], (C, LANES)),
                           jnp.broadcast_to(gc_all[:, la + 2 * p + 1:la + 2 * p + 2], (C, LANES)))
                 for p in range(gw // 2)], axis=1)
            g_row = jnp.concatenate(
                [pair_t[la + 2 * p:la + 2 * p + 1, :] for p in range(gw // 2)], axis=1)
            decay = jnp.where(causal, jnp.exp(jnp.where(causal, g_col - g_row, 0.0)), 0.0)

            q = head_l2(conv_silu(q_ref, cwq_ref, 0, r0, cs, first), D ** -0.5)
            k = head_l2(conv_silu(k_ref, cwk_ref, 1, r0, cs, first), 1.0)
            v = conv_silu(v_ref, cwv_ref, 2, r0, cs, first)
            beta = head_cols(beta_all, h0)
            eg = head_cols(eg_all, la)
            kb = k * beta
            raw = lax.dot_general(jnp.concatenate([kb, q], axis=0).astype(BF16),
                                  block_diag(k.astype(BF16), C, D),
                                  (((1,), (1,)), ((), ())), preferred_element_type=F32)
            nil = jnp.where(strict, -(raw[:C] * decay), 0.0)
            yield
            inv = eye + nil
            pw = dot3_wide(nil, nil)
            yield
            for _ in range(4):
                both = dot3_wide(jnp.concatenate([inv, pw], axis=0), pw)
                inv = inv + both[:C]
                pw = both[C:]
                yield
            inv = inv + dot3_wide(inv, pw)
            yield
            rhs = jnp.concatenate([v * beta, kb * eg], axis=1).astype(BF16)
            uw = _dot(inv.astype(BF16), block_diag(rhs, C, D))
            u_s[par, c, :, cs] = uw[:, :WD]
            wq_s[par, c, :, cs] = jnp.concatenate([uw[:, WD:], q * eg], axis=0).astype(BF16)
            kd = k * head_cols(ekd_all, la)
            kd_t = jnp.concatenate(
                [jnp.concatenate([kd[:, (2 * p) * D:(2 * p + 1) * D],
                                  kd[:, (2 * p + 1) * D:(2 * p + 2) * D]], axis=0).T
                 for p in range(gw // 2)], axis=1)
            ak_s[par, c, :, h0 * C:(h0 + gw) * C] = jnp.concatenate(
                [raw[C:] * decay, kd_t], axis=0).astype(BF16)
            egl_s[par, c, :, cs] = head_cols(egl_all, la)
            yield

    def scans(first_chunk, par):
        for c in range(unroll):
            n = first_chunk + c
            r0 = n * C if isinstance(n, int) else pl.multiple_of(n * C, C)
            for wi in range(n_wide):
                h0 = wi * gw
                cs = slice(h0 * D, (h0 + gw) * D)
                state = state_s[:, cs]
                r1 = _dot(wq_s[par, c, :, cs], block_diag(state.astype(BF16), D, D))
                v_new = u_s[par, c, :, cs] - r1[:C]
                yield
                r2 = _dot(ak_s[par, c, :, h0 * C:(h0 + gw) * C],
                          block_diag(v_new.astype(BF16), C, D))
                state_s[:, cs] = state * egl_s[par, c, 0:1, cs] + r2[C:]
                o = r1[C:] + r2[:C]
                gate = _silu(z_ref[pl.ds(r0, C), cs])
                for i in range(gw):
                    hs = slice(i * D, (i + 1) * D)
                    o_ref[pl.ds(r0, C), (h0 + i) * D:(h0 + i + 1) * D] = (
                        _rms(o[:, hs], ng_ref[...]) * gate[:, hs])
                yield

    n_groups = nc // unroll
    _run_interleaved([local(c, 0, c, first=(c == 0)) for c in range(unroll)])

    def body(m, carry):
        nxt = (m + 1) * unroll
        _run_interleaved([scans(m * unroll, m % 2)]
                         + [local(nxt + c, (m + 1) % 2, c) for c in range(unroll)])
        return carry

    lax.fori_loop(0, n_groups - 1, body, 0)
    _run_interleaved([scans((n_groups - 1) * unroll, (n_groups - 1) % 2)])

    prev_s[0] = q_ref[tile - SUBLANES:tile, :]
    prev_s[1] = k_ref[tile - SUBLANES:tile, :]
    prev_s[2] = v_ref[tile - SUBLANES:tile, :]


def _deltanet(proj, dn_conv_w, alog_row, dtb_row, dn_norm_g, batch, seq):
    D = DN_HEAD_DIM
    H = DN_HEADS
    hg = DN_GROUP
    ng = H // hg
    tile = DN_TILE
    nt = seq // tile
    wid = hg * D
    slots = DN_UNROLL
    col_blk = lambda off: pl.BlockSpec((tile, wid), lambda b, g, t, off=off: (b * nt + t, off + g))
    cw_blk = lambda off: pl.BlockSpec((DN_CONV_W, wid), lambda b, g, t, off=off: (0, off + g))
    gate_blk = 6 * 1024 // LANES
    return pl.pallas_call(
        functools.partial(_dn_kernel, tile=tile, hg=hg, gw=DN_WIDE, unroll=DN_UNROLL),
        grid=(batch, ng, nt),
        in_specs=[col_blk(0), col_blk(ng), col_blk(2 * ng), col_blk(3 * ng),
                  pl.BlockSpec((tile, LANES), lambda b, g, t: (b * nt + t, gate_blk + g)),
                  cw_blk(0), cw_blk(ng), cw_blk(2 * ng),
                  pl.BlockSpec((1, LANES), lambda b, g, t: (0, g)),
                  pl.BlockSpec((1, LANES), lambda b, g, t: (0, g)),
                  pl.BlockSpec((1, D), lambda b, g, t: (0, 0))],
        out_specs=pl.BlockSpec((tile, wid), lambda b, g, t: (b * nt + t, g)),
        out_shape=jax.ShapeDtypeStruct((batch * seq, H * D), F32),
        scratch_shapes=[pltpu.VMEM((D, wid), F32),
                        pltpu.VMEM((3, SUBLANES, wid), F32),
                        pltpu.VMEM((2, slots, DN_CHUNK, wid), F32),
                        pltpu.VMEM((2, slots, 2 * DN_CHUNK, wid), BF16),
                        pltpu.VMEM((2, slots, DN_CHUNK + D, hg * DN_CHUNK), BF16),
                        pltpu.VMEM((2, slots, SUBLANES, wid), F32)],
        compiler_params=_params("parallel", "parallel", "arbitrary"),
        name="deltanet",
    )(proj, proj, proj, proj, proj, dn_conv_w, dn_conv_w, dn_conv_w,
      alog_row, dtb_row, dn_norm_g)


CF_HALO = 32
CF_ROWS = 64
CF_LN_ROWS = 32


def _conformer_kernel(a_ref, gt_ref, ah_ref, gh_ref, w_ref, b_ref, lg_ref, lb_ref, o_ref,
                      buf, ybuf, *, tt, ch):
    t = pl.program_id(1)
    halo = ah_ref[...] * jax.nn.sigmoid(gh_ref[...])
    buf[0:CF_HALO, :] = jnp.where(t > 0, halo, 0.0)
    buf[CF_HALO:, :] = a_ref[...] * jax.nn.sigmoid(gt_ref[...])

    win_rows = CF_ROWS + CF_HALO
    lead = CF_HALO - (CF_CONV_W - 1)

    def conv_cols(r0):
        for cb in range(ch // LANES):
            cs = slice(cb * LANES, (cb + 1) * LANES)
            win = buf[pl.ds(r0, win_rows), cs]
            acc = buf[pl.ds(r0 + CF_HALO, CF_ROWS), cs] * w_ref[CF_CONV_W - 1:CF_CONV_W, cs]
            for sub in range(SUBLANES):
                wb = win if sub == 0 else pltpu.roll(win, win_rows - sub, axis=0)
                for blk in range(CF_HALO // SUBLANES):
                    j = blk * SUBLANES + sub - lead
                    if 0 <= j < CF_CONV_W - 1:
                        acc = acc + wb[blk * SUBLANES:blk * SUBLANES + CF_ROWS, :] * w_ref[j:j + 1, cs]
            ybuf[pl.ds(r0, CF_ROWS), cs] = acc + b_ref[:, cs]
            yield

    def norm_rows(r0):
        y = ybuf[pl.ds(r0, CF_LN_ROWS), :]
        mu = jnp.mean(y, axis=-1, keepdims=True)
        yield
        yc = y - mu
        var = jnp.mean(yc * yc, axis=-1, keepdims=True)
        yield
        yn = yc * lax.rsqrt(var + EPS) * lg_ref[...] + lb_ref[...]
        yield
        o_ref[pl.ds(r0, CF_LN_ROWS), :] = _silu(yn).astype(o_ref.dtype)
        yield

    def norm_block(r0):
        return [norm_rows(r0 + s) for s in range(0, CF_ROWS, CF_LN_ROWS)]

    _run_interleaved([conv_cols(0)])

    def body(i, carry):
        r0 = pl.multiple_of(i * CF_ROWS, CF_ROWS)
        r_prev = pl.multiple_of((i - 1) * CF_ROWS, CF_ROWS)
        _run_interleaved(norm_block(r_prev) + [conv_cols(r0)])
        return carry

    lax.fori_loop(1, tt // CF_ROWS, body, 0)
    _run_interleaved(norm_block(tt - CF_ROWS))


def _conformer(proj, cf_dw_w, cf_dw_b, cf_ln_g, cf_ln_b, batch, seq, tt):
    ch = cf_dw_w.shape[1]
    nt = seq // tt
    a_col = 4 * 1024 // ch
    g_col = a_col + 1
    hb = tt // CF_HALO
    main = lambda c: pl.BlockSpec((tt, ch), lambda b, t, c=c: (b * nt + t, c))
    halo = lambda c: pl.BlockSpec(
        (CF_HALO, ch), lambda b, t, c=c: (jnp.maximum((b * nt + t) * hb - 1, 0), c))
    row = lambda r: pl.BlockSpec((r, ch), lambda b, t: (0, 0))
    return pl.pallas_call(
        functools.partial(_conformer_kernel, tt=tt, ch=ch),
        grid=(batch, nt),
        in_specs=[main(a_col), main(g_col), halo(a_col), halo(g_col),
                  row(CF_CONV_W), row(1), row(1), row(1)],
        out_specs=pl.BlockSpec((tt, ch), lambda b, t: (b * nt + t, 0)),
        out_shape=jax.ShapeDtypeStruct((batch * seq, ch), F32),
        scratch_shapes=[pltpu.VMEM((tt + CF_HALO, ch), F32), pltpu.VMEM((tt, ch), F32)],
        compiler_params=_params("parallel", "parallel"),
        name="conformer",
    )(proj, proj, proj, proj, cf_dw_w, cf_dw_b, cf_ln_g, cf_ln_b)


def _out_proj_kernel(x_ref, o_ref, c_ref, wo_ref, wc_ref, h_ref):
    acc = _dot(o_ref[...].astype(BF16), wo_ref[...])
    acc = acc + _dot(c_ref[...].astype(BF16), wc_ref[...])
    h_ref[...] = x_ref[...] + acc


def _out_proj(x, o, c, w, tm):
    m, d = x.shape
    ko, kc = o.shape[1], c.shape[1]
    assert ko == kc and w.shape[0] == ko + kc
    w_rows = lambda blk: pl.BlockSpec((ko, d), lambda i, blk=blk: (blk, 0),
                                      pipeline_mode=pl.Buffered(1))
    return pl.pallas_call(
        _out_proj_kernel,
        grid=(m // tm,),
        in_specs=[pl.BlockSpec((tm, d), lambda i: (i, 0)),
                  pl.BlockSpec((tm, ko), lambda i: (i, 0)),
                  pl.BlockSpec((tm, kc), lambda i: (i, 0)),
                  w_rows(0), w_rows(1)],
        out_specs=pl.BlockSpec((tm, d), lambda i: (i, 0)),
        out_shape=jax.ShapeDtypeStruct((m, d), F32),
        compiler_params=_params("parallel"),
        name="out_proj",
    )(x, o, c, w, w)


def _xattn_kernel(h_ref, g_ref, wq_ref, kv_ref, wo_ref, o_ref, *, d_model):
    dh = d_model // XA_HEADS
    h1 = h_ref[...]
    hn = _rms(h1, g_ref[...]).astype(BF16)
    q = _dot(hn, wq_ref[...]).astype(BF16)
    heads = []
    for hd in range(XA_HEADS):
        cs = slice(hd * dh, (hd + 1) * dh)
        k = kv_ref[:, cs]
        v = kv_ref[:, d_model + hd * dh:d_model + (hd + 1) * dh]
        s = lax.dot_general(q[:, cs], k, (((1,), (1,)), ((), ())),
                            preferred_element_type=F32) * (dh ** -0.5)
        p = jnp.exp(s - jnp.max(s, axis=-1, keepdims=True))
        p = p / jnp.sum(p, axis=-1, keepdims=True)
        heads.append(_dot(p.astype(BF16), v).astype(BF16))
    o = jnp.concatenate(heads, axis=-1)
    o_ref[...] = h1 + _dot(o, wo_ref[...])


def _xattn(h1, g, wq, kv, wo, batch, seq, tm):
    m, d = h1.shape
    nt = seq // tm
    n_mem = kv.shape[0] // batch
    const = lambda shape: pl.BlockSpec(shape, lambda b, t: (0, 0), pipeline_mode=pl.Buffered(1))
    return pl.pallas_call(
        functools.partial(_xattn_kernel, d_model=d),
        grid=(batch, nt),
        in_specs=[pl.BlockSpec((tm, d), lambda b, t: (b * nt + t, 0)),
                  const((1, d)), const((d, d)),
                  pl.BlockSpec((n_mem, 2 * d), lambda b, t: (b, 0)),
                  const((d, d))],
        out_specs=pl.BlockSpec((tm, d), lambda b, t: (b * nt + t, 0)),
        out_shape=jax.ShapeDtypeStruct((m, d), F32),
        compiler_params=_params("parallel", "parallel"),
        name="xattn",
    )(h1, g, wq, kv, wo)


def _mlp_kernel(h_ref, g_ref, w1_ref, w2_ref, gf_ref, o_ref, hn_s, acc_s):
    j = pl.program_id(1)

    @pl.when(j == 0)
    def _():
        hn_s[...] = _rms(h_ref[...], g_ref[...]).astype(BF16)
        acc_s[...] = jnp.zeros_like(acc_s)

    hid = jnp.maximum(_dot(hn_s[...], w1_ref[...]), 0.0)
    acc_s[...] += _dot((hid * hid).astype(BF16), w2_ref[...])

    @pl.when(j == pl.num_programs(1) - 1)
    def _():
        o_ref[...] = _rms(h_ref[...] + acc_s[...], gf_ref[...])


def _mlp(h2, g, w1, w2, gf, tm, tf):
    m, d = h2.shape
    ff = w1.shape[1]
    return pl.pallas_call(
        _mlp_kernel,
        grid=(m // tm, ff // tf),
        in_specs=[pl.BlockSpec((tm, d), lambda i, j: (i, 0)),
                  pl.BlockSpec((1, d), lambda i, j: (0, 0)),
                  pl.BlockSpec((d, tf), lambda i, j: (0, j)),
                  pl.BlockSpec((tf, d), lambda i, j: (j, 0)),
                  pl.BlockSpec((1, d), lambda i, j: (0, 0))],
        out_specs=pl.BlockSpec((tm, d), lambda i, j: (i, 0)),
        out_shape=jax.ShapeDtypeStruct((m, d), F32),
        scratch_shapes=[pltpu.VMEM((tm, d), BF16), pltpu.VMEM((tm, d), F32)],
        compiler_params=_params("parallel", "arbitrary"),
        name="mlp",
    )(h2, g, w1, w2, gf)


def kernel(x, mem, norm_mix_g, w_in, dn_conv_w, dn_a_log, dn_dt_bias, dn_norm_g, cf_dw_w, cf_dw_b, cf_ln_g, cf_ln_b, w_out, norm_xa_g, norm_mem_g, xa_wq, xa_wk, xa_wv, xa_wo, norm_mlp_g, mlp_w1, mlp_w2, norm_final_g):
    batch, seq, d = x.shape
    n_mem = mem.shape[1]
    d_delta = DN_HEADS * DN_HEAD_DIM
    assert w_in.shape[0] == 1, "single-layer block"
    row = lambda v: v.reshape(1, -1).astype(F32)

    w_in0 = w_in[0]
    n_gate = 2 * DN_HEADS
    hg = DN_GROUP
    n_grp = DN_HEADS // hg
    w_beta = w_in0[:, 4 * d_delta:4 * d_delta + DN_HEADS]
    w_alpha = w_in0[:, 4 * d_delta + DN_HEADS:4 * d_delta + n_gate]
    gate_cols = []
    for grp in range(n_grp):
        hs = slice(grp * hg, (grp + 1) * hg)
        gate_cols += [w_beta[:, hs], w_alpha[:, hs], jnp.zeros((d, LANES - 2 * hg), F32)]
    w_proj = jnp.concatenate(
        [piece.astype(BF16) for piece in
         [w_in0[:, :4 * d_delta], w_in0[:, 4 * d_delta + n_gate:]] + gate_cols], axis=1)

    def pad_gate(vec):
        v2 = vec.astype(F32).reshape(n_grp, hg)
        return jnp.pad(v2, ((0, 0), (hg, LANES - 2 * hg))).reshape(1, n_grp * LANES)

    x2 = x.reshape(batch * seq, d)
    proj = _norm_matmul(x2, row(norm_mix_g[0]), w_proj, tm=1024, tn=1280)

    o = _deltanet(proj, dn_conv_w[0], pad_gate(dn_a_log[0]), pad_gate(dn_dt_bias[0]),
                  row(dn_norm_g[0]), batch, seq)
    c = _conformer(proj, cf_dw_w[0], row(cf_dw_b[0]), row(cf_ln_g[0]), row(cf_ln_b[0]),
                   batch, seq, tt=512)

    h1 = _out_proj(x2, o, c, w_out[0].astype(BF16), tm=512)

    w_kv = jnp.concatenate([xa_wk[0].astype(BF16), xa_wv[0].astype(BF16)], axis=1)
    kv = _norm_matmul(mem.reshape(batch * n_mem, d), row(norm_mem_g[0]), w_kv,
                      tm=256, tn=1024, out_dtype=BF16)
    h2 = _xattn(h1, row(norm_xa_g[0]), xa_wq[0].astype(BF16), kv, xa_wo[0].astype(BF16),
                batch, seq, tm=512)

    out = _mlp(h2, row(norm_mlp_g[0]), mlp_w1[0].astype(BF16), mlp_w2[0].astype(BF16),
               row(norm_final_g), tm=512, tf=1024)
    return out.reshape(batch, seq, d)
```

```python
import functools

import jax
import jax.numpy as jnp
from jax import lax
from jax.experimental import pallas as pl
from jax.experimental.pallas import tpu as pltpu

F32 = jnp.float32
BF16 = jnp.bfloat16

EPS = 1e-6
DN_HEADS = 8
DN_HEAD_DIM = 128
DN_CONV_W = 4
DN_CHUNK = 64
CF_CONV_W = 31
XA_HEADS = 4
LANES = 128
SUBLANES = 8
VMEM_LIMIT = 56 * 1024 * 1024


def _params(*sem):
    return pltpu.CompilerParams(dimension_semantics=sem, vmem_limit_bytes=VMEM_LIMIT)


def _rms(x, g):
    return x * lax.rsqrt(jnp.mean(x * x, axis=-1, keepdims=True) + EPS) * g


def _silu(x):
    h = 0.5 * x
    return h + h * jnp.tanh(h)


def _softplus(x):
    return jnp.maximum(x, 0.0) + jnp.log1p(jnp.exp(-jnp.abs(x)))


def _dot(a, b):
    return jnp.dot(a, b, preferred_element_type=F32)


def _run_interleaved(gens):
    live = list(gens)
    while live:
        nxt = []
        for g in live:
            try:
                next(g)
                nxt.append(g)
            except StopIteration:
                pass
        live = nxt


def _split(a):
    hi = a.astype(BF16)
    return hi, (a - hi.astype(F32)).astype(BF16)


def _norm_matmul_kernel(x_ref, g_ref, w_ref, o_ref, xn_ref):
    @pl.when(pl.program_id(1) == 0)
    def _():
        xn_ref[...] = _rms(x_ref[...], g_ref[...]).astype(BF16)

    o_ref[...] = _dot(xn_ref[...], w_ref[...]).astype(o_ref.dtype)


def _norm_matmul(x, g, w, tm, tn, out_dtype=F32):
    m, k = x.shape
    n = w.shape[1]
    return pl.pallas_call(
        _norm_matmul_kernel,
        grid=(m // tm, n // tn),
        in_specs=[pl.BlockSpec((tm, k), lambda i, j: (i, 0)),
                  pl.BlockSpec((1, k), lambda i, j: (0, 0)),
                  pl.BlockSpec((k, tn), lambda i, j: (0, j))],
        out_specs=pl.BlockSpec((tm, tn), lambda i, j: (i, j)),
        out_shape=jax.ShapeDtypeStruct((m, n), out_dtype),
        scratch_shapes=[pltpu.VMEM((tm, k), BF16)],
        compiler_params=_params("parallel", "arbitrary"),
        name="norm_matmul",
    )(x, g, w)


W_BLK = 2 * LANES


def _layout_w_kernel(a_ref, b_ref, g_ref, o_ref, *, lead_blks, tail_blks, shift):
    j = pl.program_id(0)

    @pl.when(j < lead_blks)
    def _():
        o_ref[...] = a_ref[...]

    @pl.when(jnp.logical_and(j >= lead_blks, j < lead_blks + tail_blks))
    def _():
        o_ref[...] = jnp.concatenate([a_ref[:, shift:], b_ref[:, :shift]], axis=1)

    @pl.when(j >= lead_blks + tail_blks)
    def _():
        o_ref[...] = g_ref[...]


def _layout_w_in(w, w_gate, n_lead, n_skip, n_tail):
    k = w.shape[0]
    assert n_lead % W_BLK == 0 and n_tail % W_BLK == 0 and 0 < n_skip < W_BLK
    assert w_gate.shape == (k, W_BLK)
    lead_blks, tail_blks = n_lead // W_BLK, n_tail // W_BLK
    last_in = (w.shape[1] - 1) // W_BLK
    n_out = lead_blks + tail_blks + 1
    return pl.pallas_call(
        functools.partial(_layout_w_kernel, lead_blks=lead_blks, tail_blks=tail_blks, shift=n_skip),
        grid=(n_out,),
        in_specs=[pl.BlockSpec((k, W_BLK), lambda j: (0, jnp.minimum(j, last_in))),
                  pl.BlockSpec((k, W_BLK), lambda j: (0, jnp.minimum(j + 1, last_in))),
                  pl.BlockSpec((k, W_BLK), lambda j: (0, 0))],
        out_specs=pl.BlockSpec((k, W_BLK), lambda j: (0, j)),
        out_shape=jax.ShapeDtypeStruct((k, n_out * W_BLK), BF16),
        compiler_params=_params("parallel"),
        name="layout_w_in",
    )(w, w, w_gate)


DN_GROUP = 4
DN_WIDE = 4
DN_UNROLL = 4
DN_TILE = 2048


def _dn_kernel(q_ref, k_ref, v_ref, z_ref, gt_ref, cwq_ref, cwk_ref, cwv_ref,
               alog_ref, dtb_ref, ng_ref, o_ref,
               state_s, prev_s, u_s, wq_s, ak_s, egl_s, *, tile, hg, gw, unroll):
    C = DN_CHUNK
    D = DN_HEAD_DIM
    t = pl.program_id(2)
    nc = tile // C
    WD = gw * D
    WC = gw * C
    n_wide = hg // gw

    @pl.when(t == 0)
    def _():
        state_s[...] = jnp.zeros_like(state_s)
        prev_s[...] = jnp.zeros_like(prev_s)

    row_l = lax.broadcasted_iota(jnp.int32, (C, LANES), 0)
    first_half = lax.broadcasted_iota(jnp.int32, (C, LANES), 1) < C
    row_c = lax.broadcasted_iota(jnp.int32, (C, WC), 0)
    col_c = lax.broadcasted_iota(jnp.int32, (C, WC), 1) % C
    causal = row_c >= col_c
    strict = row_c > col_c
    eye = (row_c == col_c).astype(F32)

    def block_diag(x, rows_per_head, cols_per_head):
        tall = jnp.concatenate([x] * gw, axis=0)
        rb = lax.broadcasted_iota(jnp.int32, tall.shape, 0) // rows_per_head
        cb = (lax.broadcasted_iota(jnp.int32, tall.shape, 1) // cols_per_head) % gw
        return jnp.where(rb == cb, tall, jnp.zeros_like(tall))

    def head_cols(mat, lane0):
        return jnp.concatenate(
            [jnp.broadcast_to(mat[:, lane0 + i:lane0 + i + 1], (mat.shape[0], D)) for i in range(gw)],
            axis=1)

    def head_l2(x, scale):
        parts = []
        for i in range(gw):
            xi = x[:, i * D:(i + 1) * D]
            parts.append(xi * (lax.rsqrt(jnp.sum(xi * xi, axis=-1, keepdims=True) + EPS) * scale))
        return jnp.concatenate(parts, axis=1)

    def conv_silu(x_ref, w_ref, which, r0, cs, first):
        x = x_ref[pl.ds(r0, C), cs]
        if first:
            prev = prev_s[which, :, cs]
        else:
            prev = x_ref[pl.ds(r0 - SUBLANES, SUBLANES), cs]
        xw = jnp.concatenate([prev, x], axis=0)
        acc = x * w_ref[DN_CONV_W - 1:DN_CONV_W, cs]
        for j in range(DN_CONV_W - 1):
            shifted = pltpu.roll(xw, DN_CONV_W - 1 - j, axis=0)[SUBLANES:, :]
            acc = acc + shifted * w_ref[j:j + 1, cs]
        return _silu(acc)

    def dot3_wide(lhs, rhs):
        lh, ll = _split(lhs)
        rh, rl = _split(rhs)
        bh = block_diag(rh, C, C)
        bl = block_diag(rl, C, C)
        return _dot(jnp.concatenate([lh, lh, ll], axis=1), jnp.concatenate([bh, bl, bh], axis=0))

    def gate_terms(n):
        r0 = n * C if isinstance(n, int) else pl.multiple_of(n * C, C)
        gt = gt_ref[pl.ds(r0, C), :]
        beta_all = jax.nn.sigmoid(gt)
        gc_all = -jnp.exp(alog_ref[...]) * _softplus(gt + dtb_ref[...])
        for s in (1, 2, 4, 8, 16, 32):
            gc_all = gc_all + jnp.where(row_l >= s, pltpu.roll(gc_all, s, axis=0), 0.0)
        g_last_all = gc_all[C - 1:C, :]
        eg_all = jnp.exp(gc_all)
        ekd_all = jnp.exp(g_last_all - gc_all)
        egl_all = jnp.broadcast_to(jnp.exp(g_last_all), (SUBLANES, LANES))
        pair_t = jnp.concatenate([gc_all, pltpu.roll(gc_all, LANES - 1, axis=1)], axis=0).T
        return beta_all, gc_all, eg_all, ekd_all, egl_all, pair_t

    def local(n, par, c, wis, terms, first=False):
        r0 = n * C if isinstance(n, int) else pl.multiple_of(n * C, C)
        beta_all, gc_all, eg_all, ekd_all, egl_all, pair_t = terms
        for wi in wis:
            h0 = wi * gw
            cs = slice(h0 * D, (h0 + gw) * D)
            la = hg + h0
            g_col = jnp.concatenate(
                [jnp.where(first_half,
                           jnp.broadcast_to(gc_all[:, la + 2 * p:la + 2 * p + 1], (C, LANES)),
                           jnp.broadcast_to(gc_all[:, la + 2 * p + 1:la + 2 * p + 2], (C, LANES)))
                 for p in range(gw // 2)], axis=1)
            g_row = jnp.concatenate(
                [pair_t[la + 2 * p:la + 2 * p + 1, :] for p in range(gw // 2)], axis=1)
            decay = jnp.where(causal, jnp.exp(jnp.where(causal, g_col - g_row, 0.0)), 0.0)

            q = head_l2(conv_silu(q_ref, cwq_ref, 0, r0, cs, first), D ** -0.5)
            k = head_l2(conv_silu(k_ref, cwk_ref, 1, r0, cs, first), 1.0)
            v = conv_silu(v_ref, cwv_ref, 2, r0, cs, first)
            beta = head_cols(beta_all, h0)
            eg = head_cols(eg_all, la)
            kb = k * beta
            raw = lax.dot_general(jnp.concatenate([kb, q], axis=0).astype(BF16),
                                  block_diag(k.astype(BF16), C, D),
                                  (((1,), (1,)), ((), ())), preferred_element_type=F32)
            nil = jnp.where(strict, -(raw[:C] * decay), 0.0)
            yield
            inv = eye + nil
            pw = dot3_wide(nil, nil)
            yield
            for _ in range(4):
                both = dot3_wide(jnp.concatenate([inv, pw], axis=0), pw)
                inv = inv + both[:C]
                pw = both[C:]
                yield
            inv = inv + dot3_wide(inv, pw)
            yield
            rhs = jnp.concatenate([v * beta, kb * eg], axis=1).astype(BF16)
            uw = _dot(inv.astype(BF16), block_diag(rhs, C, D))
            u_s[par, c, :, cs] = uw[:, :WD]
            wq_s[par, c, :, cs] = jnp.concatenate([uw[:, WD:], q * eg], axis=0).astype(BF16)
            kd = k * head_cols(ekd_all, la)
            kd_t = jnp.concatenate(
                [jnp.concatenate([kd[:, (2 * p) * D:(2 * p + 1) * D],
                                  kd[:, (2 * p + 1) * D:(2 * p + 2) * D]], axis=0).T
                 for p in range(gw // 2)], axis=1)
            ak_s[par, c, :, h0 * C:(h0 + gw) * C] = jnp.concatenate(
                [raw[C:] * decay, kd_t], axis=0).astype(BF16)
            egl_s[par, c, :, cs] = head_cols(egl_all, la)
            yield

    def scans(first_chunk, par, wis):
        for c in range(unroll):
            n = first_chunk + c
            r0 = n * C if isinstance(n, int) else pl.multiple_of(n * C, C)
            for wi in wis:
                h0 = wi * gw
                cs = slice(h0 * D, (h0 + gw) * D)
                state = state_s[:, cs]
                r1 = _dot(wq_s[par, c, :, cs], block_diag(state.astype(BF16), D, D))
                v_new = u_s[par, c, :, cs] - r1[:C]
                yield
                r2 = _dot(ak_s[par, c, :, h0 * C:(h0 + gw) * C],
                          block_diag(v_new.astype(BF16), C, D))
                state_s[:, cs] = state * egl_s[par, c, 0:1, cs] + r2[C:]
                o = r1[C:] + r2[:C]
                gate = _silu(z_ref[pl.ds(r0, C), cs])
                for i in range(gw):
                    hs = slice(i * D, (i + 1) * D)
                    o_ref[pl.ds(r0, C), (h0 + i) * D:(h0 + i + 1) * D] = (
                        _rms(o[:, hs], ng_ref[...]) * gate[:, hs]).astype(o_ref.dtype)
                yield

    n_groups = nc // unroll
    def locals_of(first_chunk, par, first_tile_chunk=False):
        gens = []
        for c in range(unroll):
            terms = gate_terms(first_chunk + c)
            gens += [local(first_chunk + c, par, c, (wi,), terms, first=(first_tile_chunk and c == 0))
                     for wi in range(n_wide)]
        return gens

    def scans_of(first_chunk, par):
        return [scans(first_chunk, par, (wi,)) for wi in range(n_wide)]

    _run_interleaved(locals_of(0, 0, first_tile_chunk=True))

    def body(m, carry):
        _run_interleaved(scans_of(m * unroll, m % 2) + locals_of((m + 1) * unroll, (m + 1) % 2))
        return carry

    lax.fori_loop(0, n_groups - 1, body, 0)
    _run_interleaved(scans_of((n_groups - 1) * unroll, (n_groups - 1) % 2))

    prev_s[0] = q_ref[tile - SUBLANES:tile, :]
    prev_s[1] = k_ref[tile - SUBLANES:tile, :]
    prev_s[2] = v_ref[tile - SUBLANES:tile, :]


def _deltanet(proj, dn_conv_w, alog_row, dtb_row, dn_norm_g, batch, seq):
    D = DN_HEAD_DIM
    H = DN_HEADS
    hg = DN_GROUP
    ng = H // hg
    tile = DN_TILE
    nt = seq // tile
    wid = hg * D
    slots = DN_UNROLL
    col_blk = lambda off: pl.BlockSpec((tile, wid), lambda b, g, t, off=off: (b * nt + t, off + g))
    cw_blk = lambda off: pl.BlockSpec((DN_CONV_W, wid), lambda b, g, t, off=off: (0, off + g))
    gate_blk = 6 * 1024 // LANES
    return pl.pallas_call(
        functools.partial(_dn_kernel, tile=tile, hg=hg, gw=DN_WIDE, unroll=DN_UNROLL),
        grid=(batch, ng, nt),
        in_specs=[col_blk(0), col_blk(ng), col_blk(2 * ng), col_blk(3 * ng),
                  pl.BlockSpec((tile, LANES), lambda b, g, t: (b * nt + t, gate_blk + g)),
                  cw_blk(0), cw_blk(ng), cw_blk(2 * ng),
                  pl.BlockSpec((1, LANES), lambda b, g, t: (0, g)),
                  pl.BlockSpec((1, LANES), lambda b, g, t: (0, g)),
                  pl.BlockSpec((1, D), lambda b, g, t: (0, 0))],
        out_specs=pl.BlockSpec((tile, wid), lambda b, g, t: (b * nt + t, g)),
        out_shape=jax.ShapeDtypeStruct((batch * seq, H * D), BF16),
        scratch_shapes=[pltpu.VMEM((D, wid), F32),
                        pltpu.VMEM((3, SUBLANES, wid), F32),
                        pltpu.VMEM((2, slots, DN_CHUNK, wid), F32),
                        pltpu.VMEM((2, slots, 2 * DN_CHUNK, wid), BF16),
                        pltpu.VMEM((2, slots, DN_CHUNK + D, hg * DN_CHUNK), BF16),
                        pltpu.VMEM((2, slots, SUBLANES, wid), F32)],
        compiler_params=_params("parallel", "parallel", "arbitrary"),
        name="deltanet",
    )(proj, proj, proj, proj, proj, dn_conv_w, dn_conv_w, dn_conv_w,
      alog_row, dtb_row, dn_norm_g)


CF_HALO = 32
CF_ROWS = 64
CF_LN_ROWS = 32


def _conformer_kernel(a_ref, gt_ref, ah_ref, gh_ref, w_ref, b_ref, lg_ref, lb_ref, o_ref,
                      buf, ybuf, *, tt, ch):
    t = pl.program_id(1)
    halo = ah_ref[...] * jax.nn.sigmoid(gh_ref[...])
    buf[0:CF_HALO, :] = jnp.where(t > 0, halo, 0.0)
    buf[CF_HALO:, :] = a_ref[...] * jax.nn.sigmoid(gt_ref[...])

    win_rows = CF_ROWS + CF_HALO
    lead = CF_HALO - (CF_CONV_W - 1)

    def conv_cols(r0):
        for cb in range(ch // LANES):
            cs = slice(cb * LANES, (cb + 1) * LANES)
            win = buf[pl.ds(r0, win_rows), cs]
            acc = buf[pl.ds(r0 + CF_HALO, CF_ROWS), cs] * w_ref[CF_CONV_W - 1:CF_CONV_W, cs]
            for sub in range(SUBLANES):
                wb = win if sub == 0 else pltpu.roll(win, win_rows - sub, axis=0)
                for blk in range(CF_HALO // SUBLANES):
                    j = blk * SUBLANES + sub - lead
                    if 0 <= j < CF_CONV_W - 1:
                        acc = acc + wb[blk * SUBLANES:blk * SUBLANES + CF_ROWS, :] * w_ref[j:j + 1, cs]
            ybuf[pl.ds(r0, CF_ROWS), cs] = acc + b_ref[:, cs]
            yield

    def norm_rows(r0):
        y = ybuf[pl.ds(r0, CF_LN_ROWS), :]
        mu = jnp.mean(y, axis=-1, keepdims=True)
        yield
        yc = y - mu
        var = jnp.mean(yc * yc, axis=-1, keepdims=True)
        yield
        yn = yc * lax.rsqrt(var + EPS) * lg_ref[...] + lb_ref[...]
        yield
        o_ref[pl.ds(r0, CF_LN_ROWS), :] = _silu(yn).astype(o_ref.dtype)
        yield

    def norm_block(r0):
        return [norm_rows(r0 + s) for s in range(0, CF_ROWS, CF_LN_ROWS)]

    _run_interleaved([conv_cols(0)])

    def body(i, carry):
        r0 = pl.multiple_of(i * CF_ROWS, CF_ROWS)
        r_prev = pl.multiple_of((i - 1) * CF_ROWS, CF_ROWS)
        _run_interleaved(norm_block(r_prev) + [conv_cols(r0)])
        return carry

    lax.fori_loop(1, tt // CF_ROWS, body, 0)
    _run_interleaved(norm_block(tt - CF_ROWS))


def _conformer(proj, cf_dw_w, cf_dw_b, cf_ln_g, cf_ln_b, batch, seq, tt):
    ch = cf_dw_w.shape[1]
    nt = seq // tt
    a_col = 4 * 1024 // ch
    g_col = a_col + 1
    hb = tt // CF_HALO
    main = lambda c: pl.BlockSpec((tt, ch), lambda b, t, c=c: (b * nt + t, c))
    halo = lambda c: pl.BlockSpec(
        (CF_HALO, ch), lambda b, t, c=c: (jnp.maximum((b * nt + t) * hb - 1, 0), c))
    row = lambda r: pl.BlockSpec((r, ch), lambda b, t: (0, 0))
    return pl.pallas_call(
        functools.partial(_conformer_kernel, tt=tt, ch=ch),
        grid=(batch, nt),
        in_specs=[main(a_col), main(g_col), halo(a_col), halo(g_col),
                  row(CF_CONV_W), row(1), row(1), row(1)],
        out_specs=pl.BlockSpec((tt, ch), lambda b, t: (b * nt + t, 0)),
        out_shape=jax.ShapeDtypeStruct((batch * seq, ch), BF16),
        scratch_shapes=[pltpu.VMEM((tt + CF_HALO, ch), F32), pltpu.VMEM((tt, ch), F32)],
        compiler_params=_params("parallel", "parallel"),
        name="conformer",
    )(proj, proj, proj, proj, cf_dw_w, cf_dw_b, cf_ln_g, cf_ln_b)


def _out_proj_kernel(x_ref, o_ref, c_ref, wo_ref, wc_ref, h_ref):
    acc = _dot(o_ref[...].astype(BF16), wo_ref[...])
    acc = acc + _dot(c_ref[...].astype(BF16), wc_ref[...])
    h_ref[...] = x_ref[...] + acc


def _out_proj(x, o, c, w, tm):
    m, d = x.shape
    ko, kc = o.shape[1], c.shape[1]
    assert ko == kc and w.shape[0] == ko + kc
    w_rows = lambda blk: pl.BlockSpec((ko, d), lambda i, blk=blk: (blk, 0),
                                      pipeline_mode=pl.Buffered(1))
    return pl.pallas_call(
        _out_proj_kernel,
        grid=(m // tm,),
        in_specs=[pl.BlockSpec((tm, d), lambda i: (i, 0)),
                  pl.BlockSpec((tm, ko), lambda i: (i, 0)),
                  pl.BlockSpec((tm, kc), lambda i: (i, 0)),
                  w_rows(0), w_rows(1)],
        out_specs=pl.BlockSpec((tm, d), lambda i: (i, 0)),
        out_shape=jax.ShapeDtypeStruct((m, d), F32),
        compiler_params=_params("parallel"),
        name="out_proj",
    )(x, o, c, w, w)


def _xattn_kernel(h_ref, g_ref, wq_ref, kv_ref, wo_ref, o_ref, *, d_model):
    dh = d_model // XA_HEADS
    h1 = h_ref[...]
    hn = _rms(h1, g_ref[...]).astype(BF16)
    q = _dot(hn, wq_ref[...]).astype(BF16)
    heads = []
    for hd in range(XA_HEADS):
        cs = slice(hd * dh, (hd + 1) * dh)
        k = kv_ref[:, cs]
        v = kv_ref[:, d_model + hd * dh:d_model + (hd + 1) * dh]
        s = lax.dot_general(q[:, cs], k, (((1,), (1,)), ((), ())),
                            preferred_element_type=F32) * (dh ** -0.5)
        p = jnp.exp(s - jnp.max(s, axis=-1, keepdims=True))
        p = p / jnp.sum(p, axis=-1, keepdims=True)
        heads.append(_dot(p.astype(BF16), v).astype(BF16))
    o = jnp.concatenate(heads, axis=-1)
    o_ref[...] = h1 + _dot(o, wo_ref[...])


def _xattn(h1, g, wq, kv, wo, batch, seq, tm):
    m, d = h1.shape
    nt = seq // tm
    n_mem = kv.shape[0] // batch
    const = lambda shape: pl.BlockSpec(shape, lambda b, t: (0, 0), pipeline_mode=pl.Buffered(1))
    return pl.pallas_call(
        functools.partial(_xattn_kernel, d_model=d),
        grid=(batch, nt),
        in_specs=[pl.BlockSpec((tm, d), lambda b, t: (b * nt + t, 0)),
                  const((1, d)), const((d, d)),
                  pl.BlockSpec((n_mem, 2 * d), lambda b, t: (b, 0)),
                  const((d, d))],
        out_specs=pl.BlockSpec((tm, d), lambda b, t: (b * nt + t, 0)),
        out_shape=jax.ShapeDtypeStruct((m, d), F32),
        compiler_params=_params("parallel", "parallel"),
        name="xattn",
    )(h1, g, wq, kv, wo)


def _mlp_kernel(h_ref, g_ref, w1_ref, w2_ref, gf_ref, o_ref, hn_s, acc_s):
    j = pl.program_id(1)

    @pl.when(j == 0)
    def _():
        hn_s[...] = _rms(h_ref[...], g_ref[...]).astype(BF16)
        acc_s[...] = jnp.zeros_like(acc_s)

    hid = jnp.maximum(_dot(hn_s[...], w1_ref[...]), 0.0)
    acc_s[...] += _dot((hid * hid).astype(BF16), w2_ref[...])

    @pl.when(j == pl.num_programs(1) - 1)
    def _():
        o_ref[...] = _rms(h_ref[...] + acc_s[...], gf_ref[...])


def _mlp(h2, g, w1, w2, gf, tm, tf):
    m, d = h2.shape
    ff = w1.shape[1]
    return pl.pallas_call(
        _mlp_kernel,
        grid=(m // tm, ff // tf),
        in_specs=[pl.BlockSpec((tm, d), lambda i, j: (i, 0)),
                  pl.BlockSpec((1, d), lambda i, j: (0, 0)),
                  pl.BlockSpec((d, tf), lambda i, j: (0, j)),
                  pl.BlockSpec((tf, d), lambda i, j: (j, 0)),
                  pl.BlockSpec((1, d), lambda i, j: (0, 0))],
        out_specs=pl.BlockSpec((tm, d), lambda i, j: (i, 0)),
        out_shape=jax.ShapeDtypeStruct((m, d), F32),
        scratch_shapes=[pltpu.VMEM((tm, d), BF16), pltpu.VMEM((tm, d), F32)],
        compiler_params=_params("parallel", "arbitrary"),
        name="mlp",
    )(h2, g, w1, w2, gf)


def kernel(x, mem, norm_mix_g, w_in, dn_conv_w, dn_a_log, dn_dt_bias, dn_norm_g, cf_dw_w, cf_dw_b, cf_ln_g, cf_ln_b, w_out, norm_xa_g, norm_mem_g, xa_wq, xa_wk, xa_wv, xa_wo, norm_mlp_g, mlp_w1, mlp_w2, norm_final_g):
    batch, seq, d = x.shape
    n_mem = mem.shape[1]
    d_delta = DN_HEADS * DN_HEAD_DIM
    assert w_in.shape[0] == 1, "single-layer block"
    row = lambda v: v.reshape(1, -1).astype(F32)

    w_in0 = w_in[0]
    n_gate = 2 * DN_HEADS
    hg = DN_GROUP
    n_grp = DN_HEADS // hg
    w_beta = w_in0[:, 4 * d_delta:4 * d_delta + DN_HEADS]
    w_alpha = w_in0[:, 4 * d_delta + DN_HEADS:4 * d_delta + n_gate]
    gate_cols = []
    for grp in range(n_grp):
        hs = slice(grp * hg, (grp + 1) * hg)
        gate_cols += [w_beta[:, hs], w_alpha[:, hs], jnp.zeros((d, LANES - 2 * hg), F32)]
    gate_cols += [jnp.zeros((d, 2 * LANES - n_grp * LANES), F32)] * (n_grp < 2)
    w_gate = jnp.concatenate(gate_cols, axis=1).astype(BF16)
    w_proj = _layout_w_in(w_in0.astype(BF16), w_gate, n_lead=4 * d_delta, n_skip=n_gate,
                          n_tail=2 * d_delta)

    def pad_gate(vec):
        v2 = vec.astype(F32).reshape(n_grp, hg)
        return jnp.pad(v2, ((0, 0), (hg, LANES - 2 * hg))).reshape(1, n_grp * LANES)

    x2 = x.reshape(batch * seq, d)
    proj = _norm_matmul(x2, row(norm_mix_g[0]), w_proj, tm=1024, tn=1280)

    o = _deltanet(proj, dn_conv_w[0], pad_gate(dn_a_log[0]), pad_gate(dn_dt_bias[0]),
                  row(dn_norm_g[0]), batch, seq)
    c = _conformer(proj, cf_dw_w[0], row(cf_dw_b[0]), row(cf_ln_g[0]), row(cf_ln_b[0]),
                   batch, seq, tt=512)

    h1 = _out_proj(x2, o, c, w_out[0].astype(BF16), tm=512)

    w_kv = jnp.concatenate([xa_wk[0].astype(BF16), xa_wv[0].astype(BF16)], axis=1)
    kv = _norm_matmul(mem.reshape(batch * n_mem, d), row(norm_mem_g[0]), w_kv,
                      tm=256, tn=1024, out_dtype=BF16)
    h2 = _xattn(h1, row(norm_xa_g[0]), xa_wq[0].astype(BF16), kv, xa_wo[0].astype(BF16),
                batch, seq, tm=512)

    out = _mlp(h2, row(norm_mlp_g[0]), mlp_w1[0].astype(BF16), mlp_w2[0].astype(BF16),
               row(norm_final_g), tm=512, tf=1024)
    return out.reshape(batch, seq, d)
```

```python
import functools

import jax
import jax.numpy as jnp
from jax import lax
from jax.experimental import pallas as pl
from jax.experimental.pallas import tpu as pltpu

F32 = jnp.float32
BF16 = jnp.bfloat16

EPS = 1e-6
DN_HEADS = 8
DN_HEAD_DIM = 128
DN_CONV_W = 4
DN_CHUNK = 64
CF_CONV_W = 31
XA_HEADS = 4
LANES = 128
SUBLANES = 8
VMEM_LIMIT = 56 * 1024 * 1024


def _params(*sem):
    return pltpu.CompilerParams(dimension_semantics=sem, vmem_limit_bytes=VMEM_LIMIT)


def _rms(x, g):
    return x * lax.rsqrt(jnp.mean(x * x, axis=-1, keepdims=True) + EPS) * g


def _sigmoid(x):
    return 0.5 + 0.5 * jnp.tanh(0.5 * x)


def _silu(x):
    h = 0.5 * x
    return h + h * jnp.tanh(h)


def _softplus(x):
    return jnp.maximum(x, 0.0) + jnp.log1p(jnp.exp(-jnp.abs(x)))


def _dot(a, b):
    return jnp.dot(a, b, preferred_element_type=F32)


def _run_interleaved(gens):
    live = list(gens)
    while live:
        nxt = []
        for g in live:
            try:
                next(g)
                nxt.append(g)
            except StopIteration:
                pass
        live = nxt


def _split(a):
    hi = a.astype(BF16)
    return hi, (a - hi.astype(F32)).astype(BF16)


def _norm_matmul_kernel(x_ref, g_ref, w_ref, o_ref, xn_ref):
    @pl.when(pl.program_id(1) == 0)
    def _():
        xn_ref[...] = _rms(x_ref[...], g_ref[...]).astype(BF16)

    o_ref[...] = _dot(xn_ref[...], w_ref[...]).astype(o_ref.dtype)


def _norm_matmul(x, g, w, tm, tn, out_dtype=F32):
    m, k = x.shape
    n = w.shape[1]
    return pl.pallas_call(
        _norm_matmul_kernel,
        grid=(m // tm, n // tn),
        in_specs=[pl.BlockSpec((tm, k), lambda i, j: (i, 0)),
                  pl.BlockSpec((1, k), lambda i, j: (0, 0)),
                  pl.BlockSpec((k, tn), lambda i, j: (0, j))],
        out_specs=pl.BlockSpec((tm, tn), lambda i, j: (i, j)),
        out_shape=jax.ShapeDtypeStruct((m, n), out_dtype),
        scratch_shapes=[pltpu.VMEM((tm, k), BF16)],
        compiler_params=_params("parallel", "arbitrary"),
        name="norm_matmul",
    )(x, g, w)


W_BLK = 2 * LANES


def _layout_w_kernel(a_ref, b_ref, g_ref, o_ref, *, lead_blks, tail_blks, shift):
    j = pl.program_id(0)

    @pl.when(j < lead_blks)
    def _():
        o_ref[...] = a_ref[...]

    @pl.when(jnp.logical_and(j >= lead_blks, j < lead_blks + tail_blks))
    def _():
        o_ref[...] = jnp.concatenate([a_ref[:, shift:], b_ref[:, :shift]], axis=1)

    @pl.when(j >= lead_blks + tail_blks)
    def _():
        o_ref[...] = g_ref[...]


def _layout_w_in(w, w_gate, n_lead, n_skip, n_tail):
    k = w.shape[0]
    assert n_lead % W_BLK == 0 and n_tail % W_BLK == 0 and 0 < n_skip < W_BLK
    assert w_gate.shape == (k, W_BLK)
    lead_blks, tail_blks = n_lead // W_BLK, n_tail // W_BLK
    last_in = (w.shape[1] - 1) // W_BLK
    n_out = lead_blks + tail_blks + 1
    return pl.pallas_call(
        functools.partial(_layout_w_kernel, lead_blks=lead_blks, tail_blks=tail_blks, shift=n_skip),
        grid=(n_out,),
        in_specs=[pl.BlockSpec((k, W_BLK), lambda j: (0, jnp.minimum(j, last_in))),
                  pl.BlockSpec((k, W_BLK), lambda j: (0, jnp.minimum(j + 1, last_in))),
                  pl.BlockSpec((k, W_BLK), lambda j: (0, 0))],
        out_specs=pl.BlockSpec((k, W_BLK), lambda j: (0, j)),
        out_shape=jax.ShapeDtypeStruct((k, n_out * W_BLK), BF16),
        compiler_params=_params("parallel"),
        name="layout_w_in",
    )(w, w, w_gate)


DN_GROUP = 4
DN_WIDE = 4
DN_UNROLL = 4
DN_TILE = 2048


def _dn_kernel(q_ref, k_ref, v_ref, z_ref, gt_ref, cwq_ref, cwk_ref, cwv_ref,
               alog_ref, dtb_ref, ng_ref, o_ref,
               state_s, prev_s, u_s, wq_s, ak_s, egl_s, *, tile, hg, gw, unroll):
    C = DN_CHUNK
    D = DN_HEAD_DIM
    t = pl.program_id(2)
    nc = tile // C
    WD = gw * D
    WC = gw * C
    n_wide = hg // gw

    @pl.when(t == 0)
    def _():
        state_s[...] = jnp.zeros_like(state_s)
        prev_s[...] = jnp.zeros_like(prev_s)

    row_l = lax.broadcasted_iota(jnp.int32, (C, LANES), 0)
    first_half = lax.broadcasted_iota(jnp.int32, (C, LANES), 1) < C
    row_c = lax.broadcasted_iota(jnp.int32, (C, WC), 0)
    col_c = lax.broadcasted_iota(jnp.int32, (C, WC), 1) % C
    causal = row_c >= col_c
    strict = row_c > col_c
    eye = (row_c == col_c).astype(F32)

    def block_diag(x, rows_per_head, cols_per_head):
        tall = jnp.concatenate([x] * gw, axis=0)
        rb = lax.broadcasted_iota(jnp.int32, tall.shape, 0) // rows_per_head
        cb = (lax.broadcasted_iota(jnp.int32, tall.shape, 1) // cols_per_head) % gw
        return jnp.where(rb == cb, tall, jnp.zeros_like(tall))

    def head_cols(mat, lane0):
        return jnp.concatenate(
            [jnp.broadcast_to(mat[:, lane0 + i:lane0 + i + 1], (mat.shape[0], D)) for i in range(gw)],
            axis=1)

    def head_l2(x, scale):
        parts = []
        for i in range(gw):
            xi = x[:, i * D:(i + 1) * D]
            parts.append(xi * (lax.rsqrt(jnp.sum(xi * xi, axis=-1, keepdims=True) + EPS) * scale))
        return jnp.concatenate(parts, axis=1)

    def conv_silu(x_ref, w_ref, which, r0, cs, first):
        x = x_ref[pl.ds(r0, C), cs]
        if first:
            prev = prev_s[which, :, cs]
        else:
            prev = x_ref[pl.ds(r0 - SUBLANES, SUBLANES), cs]
        xw = jnp.concatenate([prev, x], axis=0)
        acc = x * w_ref[DN_CONV_W - 1:DN_CONV_W, cs]
        for j in range(DN_CONV_W - 1):
            shifted = pltpu.roll(xw, DN_CONV_W - 1 - j, axis=0)[SUBLANES:, :]
            acc = acc + shifted * w_ref[j:j + 1, cs]
        return _silu(acc)

    def dot3_wide(lhs, rhs):
        lh, ll = _split(lhs)
        rh, rl = _split(rhs)
        bh = block_diag(rh, C, C)
        bl = block_diag(rl, C, C)
        return _dot(jnp.concatenate([lh, lh, ll], axis=1), jnp.concatenate([bh, bl, bh], axis=0))

    def gate_terms(n):
        r0 = n * C if isinstance(n, int) else pl.multiple_of(n * C, C)
        gt = gt_ref[pl.ds(r0, C), :]
        beta_all = jax.nn.sigmoid(gt)
        gc_all = -jnp.exp(alog_ref[...]) * _softplus(gt + dtb_ref[...])
        for s in (1, 2, 4, 8, 16, 32):
            gc_all = gc_all + jnp.where(row_l >= s, pltpu.roll(gc_all, s, axis=0), 0.0)
        g_last_all = gc_all[C - 1:C, :]
        eg_all = jnp.exp(gc_all)
        ekd_all = jnp.exp(g_last_all - gc_all)
        egl_all = jnp.broadcast_to(jnp.exp(g_last_all), (SUBLANES, LANES))
        pair_t = jnp.concatenate([gc_all, pltpu.roll(gc_all, LANES - 1, axis=1)], axis=0).T
        return beta_all, gc_all, eg_all, ekd_all, egl_all, pair_t

    def local(n, par, c, wis, terms, first=False):
        r0 = n * C if isinstance(n, int) else pl.multiple_of(n * C, C)
        beta_all, gc_all, eg_all, ekd_all, egl_all, pair_t = terms
        for wi in wis:
            h0 = wi * gw
            cs = slice(h0 * D, (h0 + gw) * D)
            la = hg + h0
            g_col = jnp.concatenate(
                [jnp.where(first_half,
                           jnp.broadcast_to(gc_all[:, la + 2 * p:la + 2 * p + 1], (C, LANES)),
                           jnp.broadcast_to(gc_all[:, la + 2 * p + 1:la + 2 * p + 2], (C, LANES)))
                 for p in range(gw // 2)], axis=1)
            g_row = jnp.concatenate(
                [pair_t[la + 2 * p:la + 2 * p + 1, :] for p in range(gw // 2)], axis=1)
            decay = jnp.where(causal, jnp.exp(jnp.where(causal, g_col - g_row, 0.0)), 0.0)

            q = head_l2(conv_silu(q_ref, cwq_ref, 0, r0, cs, first), D ** -0.5)
            k = head_l2(conv_silu(k_ref, cwk_ref, 1, r0, cs, first), 1.0)
            v = conv_silu(v_ref, cwv_ref, 2, r0, cs, first)
            beta = head_cols(beta_all, h0)
            eg = head_cols(eg_all, la)
            kb = k * beta
            raw = lax.dot_general(jnp.concatenate([kb, q], axis=0).astype(BF16),
                                  block_diag(k.astype(BF16), C, D),
                                  (((1,), (1,)), ((), ())), preferred_element_type=F32)
            nil = jnp.where(strict, -(raw[:C] * decay), 0.0)
            yield
            inv = eye + nil
            pw = dot3_wide(nil, nil)
            yield
            for _ in range(4):
                both = dot3_wide(jnp.concatenate([inv, pw], axis=0), pw)
                inv = inv + both[:C]
                pw = both[C:]
                yield
            inv = inv + dot3_wide(inv, pw)
            yield
            rhs = jnp.concatenate([v * beta, kb * eg], axis=1).astype(BF16)
            uw = _dot(inv.astype(BF16), block_diag(rhs, C, D))
            u_s[par, c, :, cs] = uw[:, :WD]
            wq_s[par, c, :, cs] = jnp.concatenate([uw[:, WD:], q * eg], axis=0).astype(BF16)
            kd = k * head_cols(ekd_all, la)
            kd_t = jnp.concatenate(
                [jnp.concatenate([kd[:, (2 * p) * D:(2 * p + 1) * D],
                                  kd[:, (2 * p + 1) * D:(2 * p + 2) * D]], axis=0).T
                 for p in range(gw // 2)], axis=1)
            ak_s[par, c, :, h0 * C:(h0 + gw) * C] = jnp.concatenate(
                [raw[C:] * decay, kd_t], axis=0).astype(BF16)
            egl_s[par, c, :, cs] = head_cols(egl_all, la)
            yield

    def scans(first_chunk, par, wis):
        for c in range(unroll):
            n = first_chunk + c
            r0 = n * C if isinstance(n, int) else pl.multiple_of(n * C, C)
            for wi in wis:
                h0 = wi * gw
                cs = slice(h0 * D, (h0 + gw) * D)
                state = state_s[:, cs]
                r1 = _dot(wq_s[par, c, :, cs], block_diag(state.astype(BF16), D, D))
                v_new = u_s[par, c, :, cs] - r1[:C]
                yield
                r2 = _dot(ak_s[par, c, :, h0 * C:(h0 + gw) * C],
                          block_diag(v_new.astype(BF16), C, D))
                state_s[:, cs] = state * egl_s[par, c, 0:1, cs] + r2[C:]
                o = r1[C:] + r2[:C]
                gate = _silu(z_ref[pl.ds(r0, C), cs])
                for i in range(gw):
                    hs = slice(i * D, (i + 1) * D)
                    o_ref[pl.ds(r0, C), (h0 + i) * D:(h0 + i + 1) * D] = (
                        _rms(o[:, hs], ng_ref[...]) * gate[:, hs]).astype(o_ref.dtype)
                yield

    n_groups = nc // unroll
    def locals_of(first_chunk, par, first_tile_chunk=False):
        gens = []
        for c in range(unroll):
            terms = gate_terms(first_chunk + c)
            gens += [local(first_chunk + c, par, c, (wi,), terms, first=(first_tile_chunk and c == 0))
                     for wi in range(n_wide)]
        return gens

    def scans_of(first_chunk, par):
        return [scans(first_chunk, par, (wi,)) for wi in range(n_wide)]

    _run_interleaved(locals_of(0, 0, first_tile_chunk=True))

    def body(m, carry):
        _run_interleaved(scans_of(m * unroll, m % 2) + locals_of((m + 1) * unroll, (m + 1) % 2))
        return carry

    lax.fori_loop(0, n_groups - 1, body, 0)
    _run_interleaved(scans_of((n_groups - 1) * unroll, (n_groups - 1) % 2))

    prev_s[0] = q_ref[tile - SUBLANES:tile, :]
    prev_s[1] = k_ref[tile - SUBLANES:tile, :]
    prev_s[2] = v_ref[tile - SUBLANES:tile, :]


def _deltanet(proj, dn_conv_w, alog_row, dtb_row, dn_norm_g, batch, seq):
    D = DN_HEAD_DIM
    H = DN_HEADS
    hg = DN_GROUP
    ng = H // hg
    tile = DN_TILE
    nt = seq // tile
    wid = hg * D
    slots = DN_UNROLL
    col_blk = lambda off: pl.BlockSpec((tile, wid), lambda b, g, t, off=off: (b * nt + t, off + g))
    cw_blk = lambda off: pl.BlockSpec((DN_CONV_W, wid), lambda b, g, t, off=off: (0, off + g))
    gate_blk = 6 * 1024 // LANES
    return pl.pallas_call(
        functools.partial(_dn_kernel, tile=tile, hg=hg, gw=DN_WIDE, unroll=DN_UNROLL),
        grid=(batch, ng, nt),
        in_specs=[col_blk(0), col_blk(ng), col_blk(2 * ng), col_blk(3 * ng),
                  pl.BlockSpec((tile, LANES), lambda b, g, t: (b * nt + t, gate_blk + g)),
                  cw_blk(0), cw_blk(ng), cw_blk(2 * ng),
                  pl.BlockSpec((1, LANES), lambda b, g, t: (0, g)),
                  pl.BlockSpec((1, LANES), lambda b, g, t: (0, g)),
                  pl.BlockSpec((1, D), lambda b, g, t: (0, 0))],
        out_specs=pl.BlockSpec((tile, wid), lambda b, g, t: (b * nt + t, g)),
        out_shape=jax.ShapeDtypeStruct((batch * seq, H * D), BF16),
        scratch_shapes=[pltpu.VMEM((D, wid), F32),
                        pltpu.VMEM((3, SUBLANES, wid), F32),
                        pltpu.VMEM((2, slots, DN_CHUNK, wid), F32),
                        pltpu.VMEM((2, slots, 2 * DN_CHUNK, wid), BF16),
                        pltpu.VMEM((2, slots, DN_CHUNK + D, hg * DN_CHUNK), BF16),
                        pltpu.VMEM((2, slots, SUBLANES, wid), F32)],
        compiler_params=_params("parallel", "parallel", "arbitrary"),
        name="deltanet",
    )(proj, proj, proj, proj, proj, dn_conv_w, dn_conv_w, dn_conv_w,
      alog_row, dtb_row, dn_norm_g)


CF_HALO = 32
CF_ROWS = 128
CF_LN_ROWS = 32


def _conformer_kernel(a_ref, gt_ref, ah_ref, gh_ref, w_ref, b_ref, lg_ref, lb_ref, o_ref,
                      buf, ybuf, *, tt, ch):
    t = pl.program_id(1)
    halo = ah_ref[...] * _sigmoid(gh_ref[...])
    buf[0:CF_HALO, :] = jnp.where(t > 0, halo, 0.0)
    buf[CF_HALO:, :] = a_ref[...] * _sigmoid(gt_ref[...])

    win_rows = CF_ROWS + CF_HALO
    lead = CF_HALO - (CF_CONV_W - 1)

    def conv_cols(r0):
        for cb in range(ch // LANES):
            cs = slice(cb * LANES, (cb + 1) * LANES)
            win = buf[pl.ds(r0, win_rows), cs]
            acc = buf[pl.ds(r0 + CF_HALO, CF_ROWS), cs] * w_ref[CF_CONV_W - 1:CF_CONV_W, cs]
            for sub in range(SUBLANES):
                wb = win if sub == 0 else pltpu.roll(win, win_rows - sub, axis=0)
                for blk in range(CF_HALO // SUBLANES):
                    j = blk * SUBLANES + sub - lead
                    if 0 <= j < CF_CONV_W - 1:
                        acc = acc + wb[blk * SUBLANES:blk * SUBLANES + CF_ROWS, :] * w_ref[j:j + 1, cs]
            ybuf[pl.ds(r0, CF_ROWS), cs] = acc + b_ref[:, cs]
            yield

    def norm_rows(r0):
        y = ybuf[pl.ds(r0, CF_LN_ROWS), :]
        mu = jnp.mean(y, axis=-1, keepdims=True)
        yield
        yc = y - mu
        var = jnp.mean(yc * yc, axis=-1, keepdims=True)
        yield
        yn = yc * lax.rsqrt(var + EPS) * lg_ref[...] + lb_ref[...]
        yield
        o_ref[pl.ds(r0, CF_LN_ROWS), :] = _silu(yn).astype(o_ref.dtype)
        yield

    def norm_block(r0):
        return [norm_rows(r0 + s) for s in range(0, CF_ROWS, CF_LN_ROWS)]

    _run_interleaved([conv_cols(0)])

    def body(i, carry):
        r0 = pl.multiple_of(i * CF_ROWS, CF_ROWS)
        r_prev = pl.multiple_of((i - 1) * CF_ROWS, CF_ROWS)
        _run_interleaved(norm_block(r_prev) + [conv_cols(r0)])
        return carry

    lax.fori_loop(1, tt // CF_ROWS, body, 0)
    _run_interleaved(norm_block(tt - CF_ROWS))


def _conformer(proj, cf_dw_w, cf_dw_b, cf_ln_g, cf_ln_b, batch, seq, tt):
    ch = cf_dw_w.shape[1]
    nt = seq // tt
    a_col = 4 * 1024 // ch
    g_col = a_col + 1
    hb = tt // CF_HALO
    main = lambda c: pl.BlockSpec((tt, ch), lambda b, t, c=c: (b * nt + t, c))
    halo = lambda c: pl.BlockSpec(
        (CF_HALO, ch), lambda b, t, c=c: (jnp.maximum((b * nt + t) * hb - 1, 0), c))
    row = lambda r: pl.BlockSpec((r, ch), lambda b, t: (0, 0))
    return pl.pallas_call(
        functools.partial(_conformer_kernel, tt=tt, ch=ch),
        grid=(batch, nt),
        in_specs=[main(a_col), main(g_col), halo(a_col), halo(g_col),
                  row(CF_CONV_W), row(1), row(1), row(1)],
        out_specs=pl.BlockSpec((tt, ch), lambda b, t: (b * nt + t, 0)),
        out_shape=jax.ShapeDtypeStruct((batch * seq, ch), BF16),
        scratch_shapes=[pltpu.VMEM((tt + CF_HALO, ch), F32), pltpu.VMEM((tt, ch), F32)],
        compiler_params=_params("parallel", "parallel"),
        name="conformer",
    )(proj, proj, proj, proj, cf_dw_w, cf_dw_b, cf_ln_g, cf_ln_b)


def _out_proj_kernel(x_ref, o_ref, c_ref, wo_ref, wc_ref, h_ref):
    acc = _dot(o_ref[...].astype(BF16), wo_ref[...])
    acc = acc + _dot(c_ref[...].astype(BF16), wc_ref[...])
    h_ref[...] = x_ref[...] + acc


def _out_proj(x, o, c, w, tm):
    m, d = x.shape
    ko, kc = o.shape[1], c.shape[1]
    assert ko == kc and w.shape[0] == ko + kc
    w_rows = lambda blk: pl.BlockSpec((ko, d), lambda i, blk=blk: (blk, 0),
                                      pipeline_mode=pl.Buffered(1))
    return pl.pallas_call(
        _out_proj_kernel,
        grid=(m // tm,),
        in_specs=[pl.BlockSpec((tm, d), lambda i: (i, 0)),
                  pl.BlockSpec((tm, ko), lambda i: (i, 0)),
                  pl.BlockSpec((tm, kc), lambda i: (i, 0)),
                  w_rows(0), w_rows(1)],
        out_specs=pl.BlockSpec((tm, d), lambda i: (i, 0)),
        out_shape=jax.ShapeDtypeStruct((m, d), F32),
        compiler_params=_params("parallel"),
        name="out_proj",
    )(x, o, c, w, w)


def _xattn_kernel(h_ref, g_ref, wq_ref, kv_ref, wo_ref, o_ref, *, d_model):
    dh = d_model // XA_HEADS
    h1 = h_ref[...]
    hn = _rms(h1, g_ref[...]).astype(BF16)
    q = _dot(hn, wq_ref[...]).astype(BF16)
    heads = []
    for hd in range(XA_HEADS):
        cs = slice(hd * dh, (hd + 1) * dh)
        k = kv_ref[:, cs]
        v = kv_ref[:, d_model + hd * dh:d_model + (hd + 1) * dh]
        s = lax.dot_general(q[:, cs], k, (((1,), (1,)), ((), ())),
                            preferred_element_type=F32) * (dh ** -0.5)
        p = jnp.exp(s - jnp.max(s, axis=-1, keepdims=True))
        p = p / jnp.sum(p, axis=-1, keepdims=True)
        heads.append(_dot(p.astype(BF16), v).astype(BF16))
    o = jnp.concatenate(heads, axis=-1)
    o_ref[...] = h1 + _dot(o, wo_ref[...])


def _xattn(h1, g, wq, kv, wo, batch, seq, tm):
    m, d = h1.shape
    nt = seq // tm
    n_mem = kv.shape[0] // batch
    const = lambda shape: pl.BlockSpec(shape, lambda b, t: (0, 0), pipeline_mode=pl.Buffered(1))
    return pl.pallas_call(
        functools.partial(_xattn_kernel, d_model=d),
        grid=(batch, nt),
        in_specs=[pl.BlockSpec((tm, d), lambda b, t: (b * nt + t, 0)),
                  const((1, d)), const((d, d)),
                  pl.BlockSpec((n_mem, 2 * d), lambda b, t: (b, 0)),
                  const((d, d))],
        out_specs=pl.BlockSpec((tm, d), lambda b, t: (b * nt + t, 0)),
        out_shape=jax.ShapeDtypeStruct((m, d), F32),
        compiler_params=_params("parallel", "parallel"),
        name="xattn",
    )(h1, g, wq, kv, wo)


def _mlp_kernel(h_ref, g_ref, w1_ref, w2_ref, gf_ref, o_ref, hn_s, acc_s):
    j = pl.program_id(1)

    @pl.when(j == 0)
    def _():
        hn_s[...] = _rms(h_ref[...], g_ref[...]).astype(BF16)
        acc_s[...] = jnp.zeros_like(acc_s)

    hid = jnp.maximum(_dot(hn_s[...], w1_ref[...]), 0.0)
    acc_s[...] += _dot((hid * hid).astype(BF16), w2_ref[...])

    @pl.when(j == pl.num_programs(1) - 1)
    def _():
        o_ref[...] = _rms(h_ref[...] + acc_s[...], gf_ref[...])


def _mlp(h2, g, w1, w2, gf, tm, tf):
    m, d = h2.shape
    ff = w1.shape[1]
    return pl.pallas_call(
        _mlp_kernel,
        grid=(m // tm, ff // tf),
        in_specs=[pl.BlockSpec((tm, d), lambda i, j: (i, 0)),
                  pl.BlockSpec((1, d), lambda i, j: (0, 0)),
                  pl.BlockSpec((d, tf), lambda i, j: (0, j)),
                  pl.BlockSpec((tf, d), lambda i, j: (j, 0)),
                  pl.BlockSpec((1, d), lambda i, j: (0, 0))],
        out_specs=pl.BlockSpec((tm, d), lambda i, j: (i, 0)),
        out_shape=jax.ShapeDtypeStruct((m, d), F32),
        scratch_shapes=[pltpu.VMEM((tm, d), BF16), pltpu.VMEM((tm, d), F32)],
        compiler_params=_params("parallel", "arbitrary"),
        name="mlp",
    )(h2, g, w1, w2, gf)


def kernel(x, mem, norm_mix_g, w_in, dn_conv_w, dn_a_log, dn_dt_bias, dn_norm_g, cf_dw_w, cf_dw_b, cf_ln_g, cf_ln_b, w_out, norm_xa_g, norm_mem_g, xa_wq, xa_wk, xa_wv, xa_wo, norm_mlp_g, mlp_w1, mlp_w2, norm_final_g):
    batch, seq, d = x.shape
    n_mem = mem.shape[1]
    d_delta = DN_HEADS * DN_HEAD_DIM
    assert w_in.shape[0] == 1, "single-layer block"
    row = lambda v: v.reshape(1, -1).astype(F32)

    w_in0 = w_in[0]
    n_gate = 2 * DN_HEADS
    hg = DN_GROUP
    n_grp = DN_HEADS // hg
    w_beta = w_in0[:, 4 * d_delta:4 * d_delta + DN_HEADS]
    w_alpha = w_in0[:, 4 * d_delta + DN_HEADS:4 * d_delta + n_gate]
    gate_cols = []
    for grp in range(n_grp):
        hs = slice(grp * hg, (grp + 1) * hg)
        gate_cols += [w_beta[:, hs], w_alpha[:, hs], jnp.zeros((d, LANES - 2 * hg), F32)]
    gate_cols += [jnp.zeros((d, 2 * LANES - n_grp * LANES), F32)] * (n_grp < 2)
    w_gate = jnp.concatenate(gate_cols, axis=1).astype(BF16)
    w_proj = _layout_w_in(w_in0.astype(BF16), w_gate, n_lead=4 * d_delta, n_skip=n_gate,
                          n_tail=2 * d_delta)

    def pad_gate(vec):
        v2 = vec.astype(F32).reshape(n_grp, hg)
        return jnp.pad(v2, ((0, 0), (hg, LANES - 2 * hg))).reshape(1, n_grp * LANES)

    x2 = x.reshape(batch * seq, d)
    proj = _norm_matmul(x2, row(norm_mix_g[0]), w_proj, tm=1024, tn=1280)

    o = _deltanet(proj, dn_conv_w[0], pad_gate(dn_a_log[0]), pad_gate(dn_dt_bias[0]),
                  row(dn_norm_g[0]), batch, seq)
    c = _conformer(proj, cf_dw_w[0], row(cf_dw_b[0]), row(cf_ln_g[0]), row(cf_ln_b[0]),
                   batch, seq, tt=1024)

    h1 = _out_proj(x2, o, c, w_out[0].astype(BF16), tm=512)

    w_kv = jnp.concatenate([xa_wk[0].astype(BF16), xa_wv[0].astype(BF16)], axis=1)
    kv = _norm_matmul(mem.reshape(batch * n_mem, d), row(norm_mem_g[0]), w_kv,
                      tm=512, tn=2048, out_dtype=BF16)
    h2 = _xattn(h1, row(norm_xa_g[0]), xa_wq[0].astype(BF16), kv, xa_wo[0].astype(BF16),
                batch, seq, tm=512)

    out = _mlp(h2, row(norm_mlp_g[0]), mlp_w1[0].astype(BF16), mlp_w2[0].astype(BF16),
               row(norm_final_g), tm=512, tf=1024)
    return out.reshape(batch, seq, d)
```

```python
import functools

import jax
import jax.numpy as jnp
from jax import lax
from jax.experimental import pallas as pl
from jax.experimental.pallas import tpu as pltpu

F32 = jnp.float32
BF16 = jnp.bfloat16

EPS = 1e-6
DN_HEADS = 8
DN_HEAD_DIM = 128
DN_CONV_W = 4
DN_CHUNK = 64
CF_CONV_W = 31
XA_HEADS = 4
LANES = 128
SUBLANES = 8
VMEM_LIMIT = 56 * 1024 * 1024


def _params(*sem):
    return pltpu.CompilerParams(dimension_semantics=sem, vmem_limit_bytes=VMEM_LIMIT)


def _rms(x, g):
    return x * lax.rsqrt(jnp.mean(x * x, axis=-1, keepdims=True) + EPS) * g


def _sigmoid(x):
    return 0.5 + 0.5 * jnp.tanh(0.5 * x)


def _silu(x):
    h = 0.5 * x
    return h + h * jnp.tanh(h)


def _softplus(x):
    return jnp.maximum(x, 0.0) + jnp.log1p(jnp.exp(-jnp.abs(x)))


def _dot(a, b):
    return jnp.dot(a, b, preferred_element_type=F32)


def _run_interleaved(gens):
    live = list(gens)
    while live:
        nxt = []
        for g in live:
            try:
                next(g)
                nxt.append(g)
            except StopIteration:
                pass
        live = nxt


def _split(a):
    hi = a.astype(BF16)
    return hi, (a - hi.astype(F32)).astype(BF16)


def _norm_matmul_kernel(x_ref, g_ref, w_ref, o_ref, xn_ref):
    @pl.when(pl.program_id(1) == 0)
    def _():
        xn_ref[...] = _rms(x_ref[...], g_ref[...]).astype(BF16)

    o_ref[...] = _dot(xn_ref[...], w_ref[...]).astype(o_ref.dtype)


def _norm_matmul(x, g, w, tm, tn, out_dtype=F32):
    m, k = x.shape
    n = w.shape[1]
    return pl.pallas_call(
        _norm_matmul_kernel,
        grid=(m // tm, n // tn),
        in_specs=[pl.BlockSpec((tm, k), lambda i, j: (i, 0)),
                  pl.BlockSpec((1, k), lambda i, j: (0, 0)),
                  pl.BlockSpec((k, tn), lambda i, j: (0, j))],
        out_specs=pl.BlockSpec((tm, tn), lambda i, j: (i, j)),
        out_shape=jax.ShapeDtypeStruct((m, n), out_dtype),
        scratch_shapes=[pltpu.VMEM((tm, k), BF16)],
        compiler_params=_params("parallel", "arbitrary"),
        name="norm_matmul",
    )(x, g, w)


W_BLK = 2 * LANES


def _layout_w_kernel(a_ref, b_ref, g_ref, o_ref, *, lead_blks, tail_blks, shift):
    j = pl.program_id(0)

    @pl.when(j < lead_blks)
    def _():
        o_ref[...] = a_ref[...]

    @pl.when(jnp.logical_and(j >= lead_blks, j < lead_blks + tail_blks))
    def _():
        o_ref[...] = jnp.concatenate([a_ref[:, shift:], b_ref[:, :shift]], axis=1)

    @pl.when(j >= lead_blks + tail_blks)
    def _():
        o_ref[...] = g_ref[...]


def _layout_w_in(w, w_gate, n_lead, n_skip, n_tail):
    k = w.shape[0]
    assert n_lead % W_BLK == 0 and n_tail % W_BLK == 0 and 0 < n_skip < W_BLK
    assert w_gate.shape == (k, W_BLK)
    lead_blks, tail_blks = n_lead // W_BLK, n_tail // W_BLK
    last_in = (w.shape[1] - 1) // W_BLK
    n_out = lead_blks + tail_blks + 1
    return pl.pallas_call(
        functools.partial(_layout_w_kernel, lead_blks=lead_blks, tail_blks=tail_blks, shift=n_skip),
        grid=(n_out,),
        in_specs=[pl.BlockSpec((k, W_BLK), lambda j: (0, jnp.minimum(j, last_in))),
                  pl.BlockSpec((k, W_BLK), lambda j: (0, jnp.minimum(j + 1, last_in))),
                  pl.BlockSpec((k, W_BLK), lambda j: (0, 0))],
        out_specs=pl.BlockSpec((k, W_BLK), lambda j: (0, j)),
        out_shape=jax.ShapeDtypeStruct((k, n_out * W_BLK), BF16),
        compiler_params=_params("parallel"),
        name="layout_w_in",
    )(w, w, w_gate)


DN_GROUP = 4
DN_WIDE = 4
DN_UNROLL = 4
DN_TILE = 2048


def _dn_kernel(q_ref, k_ref, v_ref, z_ref, gt_ref, cwq_ref, cwk_ref, cwv_ref,
               alog_ref, dtb_ref, ng_ref, o_ref,
               state_s, prev_s, u_s, wq_s, ak_s, egl_s, *, tile, hg, gw, unroll):
    C = DN_CHUNK
    D = DN_HEAD_DIM
    t = pl.program_id(2)
    nc = tile // C
    WD = gw * D
    WC = gw * C
    n_wide = hg // gw

    @pl.when(t == 0)
    def _():
        state_s[...] = jnp.zeros_like(state_s)
        prev_s[...] = jnp.zeros_like(prev_s)

    row_l = lax.broadcasted_iota(jnp.int32, (C, LANES), 0)
    first_half = lax.broadcasted_iota(jnp.int32, (C, LANES), 1) < C
    row_c = lax.broadcasted_iota(jnp.int32, (C, WC), 0)
    col_c = lax.broadcasted_iota(jnp.int32, (C, WC), 1) % C
    causal = row_c >= col_c
    strict = row_c > col_c
    eye = (row_c == col_c).astype(F32)

    def block_diag(x, rows_per_head, cols_per_head):
        tall = jnp.concatenate([x] * gw, axis=0)
        rb = lax.broadcasted_iota(jnp.int32, tall.shape, 0) // rows_per_head
        cb = (lax.broadcasted_iota(jnp.int32, tall.shape, 1) // cols_per_head) % gw
        return jnp.where(rb == cb, tall, jnp.zeros_like(tall))

    def head_cols(mat, lane0):
        return jnp.concatenate(
            [jnp.broadcast_to(mat[:, lane0 + i:lane0 + i + 1], (mat.shape[0], D)) for i in range(gw)],
            axis=1)

    def head_l2(x, scale):
        parts = []
        for i in range(gw):
            xi = x[:, i * D:(i + 1) * D]
            parts.append(xi * (lax.rsqrt(jnp.sum(xi * xi, axis=-1, keepdims=True) + EPS) * scale))
        return jnp.concatenate(parts, axis=1)

    def conv_silu(x_ref, w_ref, which, r0, cs, first):
        x = x_ref[pl.ds(r0, C), cs]
        if first:
            prev = prev_s[which, :, cs]
        else:
            prev = x_ref[pl.ds(r0 - SUBLANES, SUBLANES), cs]
        xw = jnp.concatenate([prev, x], axis=0)
        acc = x * w_ref[DN_CONV_W - 1:DN_CONV_W, cs]
        for j in range(DN_CONV_W - 1):
            shifted = pltpu.roll(xw, DN_CONV_W - 1 - j, axis=0)[SUBLANES:, :]
            acc = acc + shifted * w_ref[j:j + 1, cs]
        return _silu(acc)

    def dot3_wide(lhs, rhs):
        lh, ll = _split(lhs)
        rh, rl = _split(rhs)
        bh = block_diag(rh, C, C)
        bl = block_diag(rl, C, C)
        return _dot(jnp.concatenate([lh, lh, ll], axis=1), jnp.concatenate([bh, bl, bh], axis=0))

    def gate_terms(n):
        r0 = n * C if isinstance(n, int) else pl.multiple_of(n * C, C)
        gt = gt_ref[pl.ds(r0, C), :]
        beta_all = jax.nn.sigmoid(gt)
        gc_all = -jnp.exp(alog_ref[...]) * _softplus(gt + dtb_ref[...])
        for s in (1, 2, 4, 8, 16, 32):
            gc_all = gc_all + jnp.where(row_l >= s, pltpu.roll(gc_all, s, axis=0), 0.0)
        g_last_all = gc_all[C - 1:C, :]
        eg_all = jnp.exp(gc_all)
        ekd_all = jnp.exp(g_last_all - gc_all)
        egl_all = jnp.broadcast_to(jnp.exp(g_last_all), (SUBLANES, LANES))
        pair_t = jnp.concatenate([gc_all, pltpu.roll(gc_all, LANES - 1, axis=1)], axis=0).T
        return beta_all, gc_all, eg_all, ekd_all, egl_all, pair_t

    def local(n, par, c, wis, terms, first=False):
        r0 = n * C if isinstance(n, int) else pl.multiple_of(n * C, C)
        beta_all, gc_all, eg_all, ekd_all, egl_all, pair_t = terms
        for wi in wis:
            h0 = wi * gw
            cs = slice(h0 * D, (h0 + gw) * D)
            la = hg + h0
            g_col = jnp.concatenate(
                [jnp.where(first_half,
                           jnp.broadcast_to(gc_all[:, la + 2 * p:la + 2 * p + 1], (C, LANES)),
                           jnp.broadcast_to(gc_all[:, la + 2 * p + 1:la + 2 * p + 2], (C, LANES)))
                 for p in range(gw // 2)], axis=1)
            g_row = jnp.concatenate(
                [pair_t[la + 2 * p:la + 2 * p + 1, :] for p in range(gw // 2)], axis=1)
            decay = jnp.where(causal, jnp.exp(jnp.where(causal, g_col - g_row, 0.0)), 0.0)

            q = head_l2(conv_silu(q_ref, cwq_ref, 0, r0, cs, first), D ** -0.5)
            k = head_l2(conv_silu(k_ref, cwk_ref, 1, r0, cs, first), 1.0)
            v = conv_silu(v_ref, cwv_ref, 2, r0, cs, first)
            beta = head_cols(beta_all, h0)
            eg = head_cols(eg_all, la)
            kb = k * beta
            raw = lax.dot_general(jnp.concatenate([kb, q], axis=0).astype(BF16),
                                  block_diag(k.astype(BF16), C, D),
                                  (((1,), (1,)), ((), ())), preferred_element_type=F32)
            nil = jnp.where(strict, -(raw[:C] * decay), 0.0)
            yield
            inv = eye + nil
            pw = dot3_wide(nil, nil)
            yield
            for _ in range(4):
                both = dot3_wide(jnp.concatenate([inv, pw], axis=0), pw)
                inv = inv + both[:C]
                pw = both[C:]
                yield
            inv = inv + dot3_wide(inv, pw)
            yield
            rhs = jnp.concatenate([v * beta, kb * eg], axis=1).astype(BF16)
            uw = _dot(inv.astype(BF16), block_diag(rhs, C, D))
            u_s[par, c, :, cs] = uw[:, :WD]
            wq_s[par, c, :, cs] = jnp.concatenate([uw[:, WD:], q * eg], axis=0).astype(BF16)
            kd = k * head_cols(ekd_all, la)
            kd_t = jnp.concatenate(
                [jnp.concatenate([kd[:, (2 * p) * D:(2 * p + 1) * D],
                                  kd[:, (2 * p + 1) * D:(2 * p + 2) * D]], axis=0).T
                 for p in range(gw // 2)], axis=1)
            ak_s[par, c, :, h0 * C:(h0 + gw) * C] = jnp.concatenate(
                [raw[C:] * decay, kd_t], axis=0).astype(BF16)
            egl_s[par, c, :, cs] = head_cols(egl_all, la)
            yield

    def scans(first_chunk, par, wis):
        for c in range(unroll):
            n = first_chunk + c
            r0 = n * C if isinstance(n, int) else pl.multiple_of(n * C, C)
            for wi in wis:
                h0 = wi * gw
                cs = slice(h0 * D, (h0 + gw) * D)
                state = state_s[:, cs]
                r1 = _dot(wq_s[par, c, :, cs], block_diag(state.astype(BF16), D, D))
                v_new = u_s[par, c, :, cs] - r1[:C]
                yield
                r2 = _dot(ak_s[par, c, :, h0 * C:(h0 + gw) * C],
                          block_diag(v_new.astype(BF16), C, D))
                state_s[:, cs] = state * egl_s[par, c, 0:1, cs] + r2[C:]
                o = r1[C:] + r2[:C]
                gate = _silu(z_ref[pl.ds(r0, C), cs])
                for i in range(gw):
                    hs = slice(i * D, (i + 1) * D)
                    o_ref[pl.ds(r0, C), (h0 + i) * D:(h0 + i + 1) * D] = (
                        _rms(o[:, hs], ng_ref[...]) * gate[:, hs]).astype(o_ref.dtype)
                yield

    n_groups = nc // unroll
    def locals_of(first_chunk, par, first_tile_chunk=False):
        gens = []
        for c in range(unroll):
            terms = gate_terms(first_chunk + c)
            gens += [local(first_chunk + c, par, c, (wi,), terms, first=(first_tile_chunk and c == 0))
                     for wi in range(n_wide)]
        return gens

    def scans_of(first_chunk, par):
        return [scans(first_chunk, par, (wi,)) for wi in range(n_wide)]

    _run_interleaved(locals_of(0, 0, first_tile_chunk=True))

    def body(m, carry):
        _run_interleaved(scans_of(m * unroll, m % 2) + locals_of((m + 1) * unroll, (m + 1) % 2))
        return carry

    lax.fori_loop(0, n_groups - 1, body, 0)
    _run_interleaved(scans_of((n_groups - 1) * unroll, (n_groups - 1) % 2))

    prev_s[0] = q_ref[tile - SUBLANES:tile, :]
    prev_s[1] = k_ref[tile - SUBLANES:tile, :]
    prev_s[2] = v_ref[tile - SUBLANES:tile, :]


def _deltanet(proj, dn_conv_w, alog_row, dtb_row, dn_norm_g, batch, seq):
    D = DN_HEAD_DIM
    H = DN_HEADS
    hg = DN_GROUP
    ng = H // hg
    tile = DN_TILE
    nt = seq // tile
    wid = hg * D
    slots = DN_UNROLL
    col_blk = lambda off: pl.BlockSpec((tile, wid), lambda b, g, t, off=off: (b * nt + t, off + g))
    cw_blk = lambda off: pl.BlockSpec((DN_CONV_W, wid), lambda b, g, t, off=off: (0, off + g))
    gate_blk = 6 * 1024 // LANES
    return pl.pallas_call(
        functools.partial(_dn_kernel, tile=tile, hg=hg, gw=DN_WIDE, unroll=DN_UNROLL),
        grid=(batch, ng, nt),
        in_specs=[col_blk(0), col_blk(ng), col_blk(2 * ng), col_blk(3 * ng),
                  pl.BlockSpec((tile, LANES), lambda b, g, t: (b * nt + t, gate_blk + g)),
                  cw_blk(0), cw_blk(ng), cw_blk(2 * ng),
                  pl.BlockSpec((1, LANES), lambda b, g, t: (0, g)),
                  pl.BlockSpec((1, LANES), lambda b, g, t: (0, g)),
                  pl.BlockSpec((1, D), lambda b, g, t: (0, 0))],
        out_specs=pl.BlockSpec((tile, wid), lambda b, g, t: (b * nt + t, g)),
        out_shape=jax.ShapeDtypeStruct((batch * seq, H * D), BF16),
        scratch_shapes=[pltpu.VMEM((D, wid), F32),
                        pltpu.VMEM((3, SUBLANES, wid), F32),
                        pltpu.VMEM((2, slots, DN_CHUNK, wid), F32),
                        pltpu.VMEM((2, slots, 2 * DN_CHUNK, wid), BF16),
                        pltpu.VMEM((2, slots, DN_CHUNK + D, hg * DN_CHUNK), BF16),
                        pltpu.VMEM((2, slots, SUBLANES, wid), F32)],
        compiler_params=_params("parallel", "parallel", "arbitrary"),
        name="deltanet",
    )(proj, proj, proj, proj, proj, dn_conv_w, dn_conv_w, dn_conv_w,
      alog_row, dtb_row, dn_norm_g)


CF_HALO = 32
CF_ROWS = 128
CF_LN_ROWS = 32


def _conformer_kernel(a_ref, gt_ref, ah_ref, gh_ref, w_ref, b_ref, lg_ref, lb_ref, o_ref,
                      buf, ybuf, *, tt, ch):
    t = pl.program_id(1)
    halo = ah_ref[...] * _sigmoid(gh_ref[...])
    buf[0:CF_HALO, :] = jnp.where(t > 0, halo, 0.0)
    buf[CF_HALO:, :] = a_ref[...] * _sigmoid(gt_ref[...])

    win_rows = CF_ROWS + CF_HALO
    lead = CF_HALO - (CF_CONV_W - 1)

    def conv_cols(r0):
        for cb in range(ch // LANES):
            cs = slice(cb * LANES, (cb + 1) * LANES)
            win = buf[pl.ds(r0, win_rows), cs]
            acc = buf[pl.ds(r0 + CF_HALO, CF_ROWS), cs] * w_ref[CF_CONV_W - 1:CF_CONV_W, cs]
            for sub in range(SUBLANES):
                wb = win if sub == 0 else pltpu.roll(win, win_rows - sub, axis=0)
                for blk in range(CF_HALO // SUBLANES):
                    j = blk * SUBLANES + sub - lead
                    if 0 <= j < CF_CONV_W - 1:
                        acc = acc + wb[blk * SUBLANES:blk * SUBLANES + CF_ROWS, :] * w_ref[j:j + 1, cs]
            ybuf[pl.ds(r0, CF_ROWS), cs] = acc + b_ref[:, cs]
            yield

    def norm_rows(r0):
        y = ybuf[pl.ds(r0, CF_LN_ROWS), :]
        mu = jnp.mean(y, axis=-1, keepdims=True)
        yield
        yc = y - mu
        var = jnp.mean(yc * yc, axis=-1, keepdims=True)
        yield
        yn = yc * lax.rsqrt(var + EPS) * lg_ref[...] + lb_ref[...]
        yield
        o_ref[pl.ds(r0, CF_LN_ROWS), :] = _silu(yn).astype(o_ref.dtype)
        yield

    def norm_block(r0):
        return [norm_rows(r0 + s) for s in range(0, CF_ROWS, CF_LN_ROWS)]

    _run_interleaved([conv_cols(0)])

    def body(i, carry):
        r0 = pl.multiple_of(i * CF_ROWS, CF_ROWS)
        r_prev = pl.multiple_of((i - 1) * CF_ROWS, CF_ROWS)
        _run_interleaved(norm_block(r_prev) + [conv_cols(r0)])
        return carry

    lax.fori_loop(1, tt // CF_ROWS, body, 0)
    _run_interleaved(norm_block(tt - CF_ROWS))


def _conformer(proj, cf_dw_w, cf_dw_b, cf_ln_g, cf_ln_b, batch, seq, tt):
    ch = cf_dw_w.shape[1]
    nt = seq // tt
    a_col = 4 * 1024 // ch
    g_col = a_col + 1
    hb = tt // CF_HALO
    main = lambda c: pl.BlockSpec((tt, ch), lambda b, t, c=c: (b * nt + t, c))
    halo = lambda c: pl.BlockSpec(
        (CF_HALO, ch), lambda b, t, c=c: (jnp.maximum((b * nt + t) * hb - 1, 0), c))
    row = lambda r: pl.BlockSpec((r, ch), lambda b, t: (0, 0))
    return pl.pallas_call(
        functools.partial(_conformer_kernel, tt=tt, ch=ch),
        grid=(batch, nt),
        in_specs=[main(a_col), main(g_col), halo(a_col), halo(g_col),
                  row(CF_CONV_W), row(1), row(1), row(1)],
        out_specs=pl.BlockSpec((tt, ch), lambda b, t: (b * nt + t, 0)),
        out_shape=jax.ShapeDtypeStruct((batch * seq, ch), BF16),
        scratch_shapes=[pltpu.VMEM((tt + CF_HALO, ch), F32), pltpu.VMEM((tt, ch), F32)],
        compiler_params=_params("parallel", "parallel"),
        name="conformer",
    )(proj, proj, proj, proj, cf_dw_w, cf_dw_b, cf_ln_g, cf_ln_b)


def _out_proj_kernel(x_ref, o_ref, c_ref, wo_ref, wc_ref, h_ref):
    acc = _dot(o_ref[...].astype(BF16), wo_ref[...])
    acc = acc + _dot(c_ref[...].astype(BF16), wc_ref[...])
    h_ref[...] = x_ref[...] + acc


def _out_proj(x, o, c, w, tm):
    m, d = x.shape
    ko, kc = o.shape[1], c.shape[1]
    assert ko == kc and w.shape[0] == ko + kc
    w_rows = lambda blk: pl.BlockSpec((ko, d), lambda i, blk=blk: (blk, 0),
                                      pipeline_mode=pl.Buffered(1))
    return pl.pallas_call(
        _out_proj_kernel,
        grid=(m // tm,),
        in_specs=[pl.BlockSpec((tm, d), lambda i: (i, 0)),
                  pl.BlockSpec((tm, ko), lambda i: (i, 0)),
                  pl.BlockSpec((tm, kc), lambda i: (i, 0)),
                  w_rows(0), w_rows(1)],
        out_specs=pl.BlockSpec((tm, d), lambda i: (i, 0)),
        out_shape=jax.ShapeDtypeStruct((m, d), F32),
        compiler_params=_params("parallel"),
        name="out_proj",
    )(x, o, c, w, w)


def _xattn_kernel(h_ref, g_ref, wq_ref, kv_ref, wo_ref, o_ref, *, d_model):
    dh = d_model // XA_HEADS
    h1 = h_ref[...]
    hn = _rms(h1, g_ref[...]).astype(BF16)
    q = _dot(hn, wq_ref[...]).astype(BF16)
    heads = [None] * XA_HEADS

    def head(hd):
        cs = slice(hd * dh, (hd + 1) * dh)
        k = kv_ref[:, cs]
        v = kv_ref[:, d_model + hd * dh:d_model + (hd + 1) * dh]
        s = lax.dot_general(q[:, cs], k, (((1,), (1,)), ((), ())),
                            preferred_element_type=F32) * (dh ** -0.5)
        yield
        p = jnp.exp(s - jnp.max(s, axis=-1, keepdims=True))
        yield
        p = p / jnp.sum(p, axis=-1, keepdims=True)
        yield
        heads[hd] = _dot(p.astype(BF16), v).astype(BF16)
        yield

    _run_interleaved([head(hd) for hd in range(XA_HEADS)])
    o = jnp.concatenate(heads, axis=-1)
    o_ref[...] = h1 + _dot(o, wo_ref[...])


def _xattn(h1, g, wq, kv, wo, batch, seq, tm):
    m, d = h1.shape
    nt = seq // tm
    n_mem = kv.shape[0] // batch
    const = lambda shape: pl.BlockSpec(shape, lambda b, t: (0, 0), pipeline_mode=pl.Buffered(1))
    return pl.pallas_call(
        functools.partial(_xattn_kernel, d_model=d),
        grid=(batch, nt),
        in_specs=[pl.BlockSpec((tm, d), lambda b, t: (b * nt + t, 0)),
                  const((1, d)), const((d, d)),
                  pl.BlockSpec((n_mem, 2 * d), lambda b, t: (b, 0)),
                  const((d, d))],
        out_specs=pl.BlockSpec((tm, d), lambda b, t: (b * nt + t, 0)),
        out_shape=jax.ShapeDtypeStruct((m, d), F32),
        compiler_params=_params("parallel", "parallel"),
        name="xattn",
    )(h1, g, wq, kv, wo)


def _mlp_kernel(h_ref, g_ref, w1_ref, w2_ref, gf_ref, o_ref, hn_s, acc_s):
    j = pl.program_id(1)

    @pl.when(j == 0)
    def _():
        hn_s[...] = _rms(h_ref[...], g_ref[...]).astype(BF16)
        acc_s[...] = jnp.zeros_like(acc_s)

    hid = jnp.maximum(_dot(hn_s[...], w1_ref[...]), 0.0)
    acc_s[...] += _dot((hid * hid).astype(BF16), w2_ref[...])

    @pl.when(j == pl.num_programs(1) - 1)
    def _():
        o_ref[...] = _rms(h_ref[...] + acc_s[...], gf_ref[...])


def _mlp(h2, g, w1, w2, gf, tm, tf):
    m, d = h2.shape
    ff = w1.shape[1]
    return pl.pallas_call(
        _mlp_kernel,
        grid=(m // tm, ff // tf),
        in_specs=[pl.BlockSpec((tm, d), lambda i, j: (i, 0)),
                  pl.BlockSpec((1, d), lambda i, j: (0, 0)),
                  pl.BlockSpec((d, tf), lambda i, j: (0, j)),
                  pl.BlockSpec((tf, d), lambda i, j: (j, 0)),
                  pl.BlockSpec((1, d), lambda i, j: (0, 0))],
        out_specs=pl.BlockSpec((tm, d), lambda i, j: (i, 0)),
        out_shape=jax.ShapeDtypeStruct((m, d), F32),
        scratch_shapes=[pltpu.VMEM((tm, d), BF16), pltpu.VMEM((tm, d), F32)],
        compiler_params=_params("parallel", "arbitrary"),
        name="mlp",
    )(h2, g, w1, w2, gf)


def kernel(x, mem, norm_mix_g, w_in, dn_conv_w, dn_a_log, dn_dt_bias, dn_norm_g, cf_dw_w, cf_dw_b, cf_ln_g, cf_ln_b, w_out, norm_xa_g, norm_mem_g, xa_wq, xa_wk, xa_wv, xa_wo, norm_mlp_g, mlp_w1, mlp_w2, norm_final_g):
    batch, seq, d = x.shape
    n_mem = mem.shape[1]
    d_delta = DN_HEADS * DN_HEAD_DIM
    assert w_in.shape[0] == 1, "single-layer block"
    row = lambda v: v.reshape(1, -1).astype(F32)

    w_in0 = w_in[0]
    n_gate = 2 * DN_HEADS
    hg = DN_GROUP
    n_grp = DN_HEADS // hg
    w_beta = w_in0[:, 4 * d_delta:4 * d_delta + DN_HEADS]
    w_alpha = w_in0[:, 4 * d_delta + DN_HEADS:4 * d_delta + n_gate]
    gate_cols = []
    for grp in range(n_grp):
        hs = slice(grp * hg, (grp + 1) * hg)
        gate_cols += [w_beta[:, hs], w_alpha[:, hs], jnp.zeros((d, LANES - 2 * hg), F32)]
    gate_cols += [jnp.zeros((d, 2 * LANES - n_grp * LANES), F32)] * (n_grp < 2)
    w_gate = jnp.concatenate(gate_cols, axis=1).astype(BF16)
    w_proj = _layout_w_in(w_in0.astype(BF16), w_gate, n_lead=4 * d_delta, n_skip=n_gate,
                          n_tail=2 * d_delta)

    def pad_gate(vec):
        v2 = vec.astype(F32).reshape(n_grp, hg)
        return jnp.pad(v2, ((0, 0), (hg, LANES - 2 * hg))).reshape(1, n_grp * LANES)

    x2 = x.reshape(batch * seq, d)
    proj = _norm_matmul(x2, row(norm_mix_g[0]), w_proj, tm=1024, tn=1280)

    o = _deltanet(proj, dn_conv_w[0], pad_gate(dn_a_log[0]), pad_gate(dn_dt_bias[0]),
                  row(dn_norm_g[0]), batch, seq)
    c = _conformer(proj, cf_dw_w[0], row(cf_dw_b[0]), row(cf_ln_g[0]), row(cf_ln_b[0]),
                   batch, seq, tt=1024)

    h1 = _out_proj(x2, o, c, w_out[0].astype(BF16), tm=512)

    w_kv = jnp.concatenate([xa_wk[0].astype(BF16), xa_wv[0].astype(BF16)], axis=1)
    kv = _norm_matmul(mem.reshape(batch * n_mem, d), row(norm_mem_g[0]), w_kv,
                      tm=512, tn=2048, out_dtype=BF16)
    h2 = _xattn(h1, row(norm_xa_g[0]), xa_wq[0].astype(BF16), kv, xa_wo[0].astype(BF16),
                batch, seq, tm=512)

    out = _mlp(h2, row(norm_mlp_g[0]), mlp_w1[0].astype(BF16), mlp_w2[0].astype(BF16),
               row(norm_final_g), tm=512, tf=1024)
    return out.reshape(batch, seq, d)
```

```python
import functools

import jax
import jax.numpy as jnp
from jax import lax
from jax.experimental import pallas as pl
from jax.experimental.pallas import tpu as pltpu

F32 = jnp.float32
BF16 = jnp.bfloat16

EPS = 1e-6
DN_HEADS = 8
DN_HEAD_DIM = 128
DN_CONV_W = 4
DN_CHUNK = 64
CF_CONV_W = 31
XA_HEADS = 4
LANES = 128
SUBLANES = 8
VMEM_LIMIT = 56 * 1024 * 1024


def _params(*sem):
    return pltpu.CompilerParams(dimension_semantics=sem, vmem_limit_bytes=VMEM_LIMIT)


def _rms(x, g):
    return x * lax.rsqrt(jnp.mean(x * x, axis=-1, keepdims=True) + EPS) * g


def _sigmoid(x):
    return 0.5 + 0.5 * jnp.tanh(0.5 * x)


def _silu(x):
    h = 0.5 * x
    return h + h * jnp.tanh(h)


def _softplus(x):
    return jnp.maximum(x, 0.0) + jnp.log1p(jnp.exp(-jnp.abs(x)))


def _dot(a, b):
    return jnp.dot(a, b, preferred_element_type=F32)


def _run_interleaved(gens):
    live = list(gens)
    while live:
        nxt = []
        for g in live:
            try:
                next(g)
                nxt.append(g)
            except StopIteration:
                pass
        live = nxt


def _split(a):
    hi = a.astype(BF16)
    return hi, (a - hi.astype(F32)).astype(BF16)


def _norm_matmul_kernel(x_ref, g_ref, w_ref, o_ref, xn_ref):
    @pl.when(pl.program_id(1) == 0)
    def _():
        xn_ref[...] = _rms(x_ref[...], g_ref[...]).astype(BF16)

    o_ref[...] = _dot(xn_ref[...], w_ref[...]).astype(o_ref.dtype)


def _norm_matmul(x, g, w, tm, tn, out_dtype=F32):
    m, k = x.shape
    n = w.shape[1]
    return pl.pallas_call(
        _norm_matmul_kernel,
        grid=(m // tm, n // tn),
        in_specs=[pl.BlockSpec((tm, k), lambda i, j: (i, 0)),
                  pl.BlockSpec((1, k), lambda i, j: (0, 0)),
                  pl.BlockSpec((k, tn), lambda i, j: (0, j))],
        out_specs=pl.BlockSpec((tm, tn), lambda i, j: (i, j)),
        out_shape=jax.ShapeDtypeStruct((m, n), out_dtype),
        scratch_shapes=[pltpu.VMEM((tm, k), BF16)],
        compiler_params=_params("parallel", "arbitrary"),
        name="norm_matmul",
    )(x, g, w)


W_BLK = 2 * LANES


def _layout_w_kernel(a_ref, b_ref, g_ref, o_ref, *, lead_blks, tail_blks, shift):
    j = pl.program_id(0)

    @pl.when(j < lead_blks)
    def _():
        o_ref[...] = a_ref[...]

    @pl.when(jnp.logical_and(j >= lead_blks, j < lead_blks + tail_blks))
    def _():
        o_ref[...] = jnp.concatenate([a_ref[:, shift:], b_ref[:, :shift]], axis=1)

    @pl.when(j >= lead_blks + tail_blks)
    def _():
        o_ref[...] = g_ref[...]


def _layout_w_in(w, w_gate, n_lead, n_skip, n_tail):
    k = w.shape[0]
    assert n_lead % W_BLK == 0 and n_tail % W_BLK == 0 and 0 < n_skip < W_BLK
    assert w_gate.shape == (k, W_BLK)
    lead_blks, tail_blks = n_lead // W_BLK, n_tail // W_BLK
    last_in = (w.shape[1] - 1) // W_BLK
    n_out = lead_blks + tail_blks + 1
    return pl.pallas_call(
        functools.partial(_layout_w_kernel, lead_blks=lead_blks, tail_blks=tail_blks, shift=n_skip),
        grid=(n_out,),
        in_specs=[pl.BlockSpec((k, W_BLK), lambda j: (0, jnp.minimum(j, last_in))),
                  pl.BlockSpec((k, W_BLK), lambda j: (0, jnp.minimum(j + 1, last_in))),
                  pl.BlockSpec((k, W_BLK), lambda j: (0, 0))],
        out_specs=pl.BlockSpec((k, W_BLK), lambda j: (0, j)),
        out_shape=jax.ShapeDtypeStruct((k, n_out * W_BLK), BF16),
        compiler_params=_params("parallel"),
        name="layout_w_in",
    )(w, w, w_gate)


DN_GROUP = 4
DN_WIDE = 4
DN_UNROLL = 4
DN_TILE = 2048


def _dn_kernel(q_ref, k_ref, v_ref, z_ref, gt_ref, cwq_ref, cwk_ref, cwv_ref,
               alog_ref, dtb_ref, ng_ref, o_ref,
               state_s, prev_s, u_s, wq_s, ak_s, egl_s, *, tile, hg, gw, unroll):
    C = DN_CHUNK
    D = DN_HEAD_DIM
    t = pl.program_id(2)
    nc = tile // C
    WD = gw * D
    WC = gw * C
    n_wide = hg // gw

    @pl.when(t == 0)
    def _():
        state_s[...] = jnp.zeros_like(state_s)
        prev_s[...] = jnp.zeros_like(prev_s)

    row_l = lax.broadcasted_iota(jnp.int32, (C, LANES), 0)
    first_half = lax.broadcasted_iota(jnp.int32, (C, LANES), 1) < C
    row_c = lax.broadcasted_iota(jnp.int32, (C, WC), 0)
    col_c = lax.broadcasted_iota(jnp.int32, (C, WC), 1) % C
    causal = row_c >= col_c
    strict = row_c > col_c
    eye = (row_c == col_c).astype(F32)

    def block_diag(x, rows_per_head, cols_per_head):
        tall = jnp.concatenate([x] * gw, axis=0)
        rb = lax.broadcasted_iota(jnp.int32, tall.shape, 0) // rows_per_head
        cb = (lax.broadcasted_iota(jnp.int32, tall.shape, 1) // cols_per_head) % gw
        return jnp.where(rb == cb, tall, jnp.zeros_like(tall))

    def head_cols(mat, lane0):
        return jnp.concatenate(
            [jnp.broadcast_to(mat[:, lane0 + i:lane0 + i + 1], (mat.shape[0], D)) for i in range(gw)],
            axis=1)

    def head_l2(x, scale):
        parts = []
        for i in range(gw):
            xi = x[:, i * D:(i + 1) * D]
            parts.append(xi * (lax.rsqrt(jnp.sum(xi * xi, axis=-1, keepdims=True) + EPS) * scale))
        return jnp.concatenate(parts, axis=1)

    def conv_silu(x_ref, w_ref, which, r0, cs, first):
        x = x_ref[pl.ds(r0, C), cs]
        if first:
            prev = prev_s[which, :, cs]
        else:
            prev = x_ref[pl.ds(r0 - SUBLANES, SUBLANES), cs]
        xw = jnp.concatenate([prev, x], axis=0)
        acc = x * w_ref[DN_CONV_W - 1:DN_CONV_W, cs]
        for j in range(DN_CONV_W - 1):
            shifted = pltpu.roll(xw, DN_CONV_W - 1 - j, axis=0)[SUBLANES:, :]
            acc = acc + shifted * w_ref[j:j + 1, cs]
        return _silu(acc)

    def dot3_wide(lhs, rhs):
        lh, ll = _split(lhs)
        rh, rl = _split(rhs)
        bh = block_diag(rh, C, C)
        bl = block_diag(rl, C, C)
        return _dot(jnp.concatenate([lh, lh, ll], axis=1), jnp.concatenate([bh, bl, bh], axis=0))

    def gate_terms(n):
        r0 = n * C if isinstance(n, int) else pl.multiple_of(n * C, C)
        gt = gt_ref[pl.ds(r0, C), :]
        beta_all = jax.nn.sigmoid(gt)
        gc_all = -jnp.exp(alog_ref[...]) * _softplus(gt + dtb_ref[...])
        for s in (1, 2, 4, 8, 16, 32):
            gc_all = gc_all + jnp.where(row_l >= s, pltpu.roll(gc_all, s, axis=0), 0.0)
        g_last_all = gc_all[C - 1:C, :]
        eg_all = jnp.exp(gc_all)
        ekd_all = jnp.exp(g_last_all - gc_all)
        egl_all = jnp.broadcast_to(jnp.exp(g_last_all), (SUBLANES, LANES))
        pair_t = jnp.concatenate([gc_all, pltpu.roll(gc_all, LANES - 1, axis=1)], axis=0).T
        return beta_all, gc_all, eg_all, ekd_all, egl_all, pair_t

    def local(n, par, c, wis, terms, first=False):
        r0 = n * C if isinstance(n, int) else pl.multiple_of(n * C, C)
        beta_all, gc_all, eg_all, ekd_all, egl_all, pair_t = terms
        for wi in wis:
            h0 = wi * gw
            cs = slice(h0 * D, (h0 + gw) * D)
            la = hg + h0
            g_col = jnp.concatenate(
                [jnp.where(first_half,
                           jnp.broadcast_to(gc_all[:, la + 2 * p:la + 2 * p + 1], (C, LANES)),
                           jnp.broadcast_to(gc_all[:, la + 2 * p + 1:la + 2 * p + 2], (C, LANES)))
                 for p in range(gw // 2)], axis=1)
            g_row = jnp.concatenate(
                [pair_t[la + 2 * p:la + 2 * p + 1, :] for p in range(gw // 2)], axis=1)
            decay = jnp.where(causal, jnp.exp(jnp.where(causal, g_col - g_row, 0.0)), 0.0)

            q = head_l2(conv_silu(q_ref, cwq_ref, 0, r0, cs, first), D ** -0.5)
            k = head_l2(conv_silu(k_ref, cwk_ref, 1, r0, cs, first), 1.0)
            v = conv_silu(v_ref, cwv_ref, 2, r0, cs, first)
            beta = head_cols(beta_all, h0)
            eg = head_cols(eg_all, la)
            kb = k * beta
            raw = lax.dot_general(jnp.concatenate([kb, q], axis=0).astype(BF16),
                                  block_diag(k.astype(BF16), C, D),
                                  (((1,), (1,)), ((), ())), preferred_element_type=F32)
            nil = jnp.where(strict, -(raw[:C] * decay), 0.0)
            yield
            inv = eye + nil
            pw = dot3_wide(nil, nil)
            yield
            for _ in range(4):
                both = dot3_wide(jnp.concatenate([inv, pw], axis=0), pw)
                inv = inv + both[:C]
                pw = both[C:]
                yield
            inv = inv + dot3_wide(inv, pw)
            yield
            rhs = jnp.concatenate([v * beta, kb * eg], axis=1).astype(BF16)
            inv16 = inv.astype(BF16)
            uw_pairs = []
            for p in range(gw // 2):
                rp = jnp.concatenate([rhs[:, p * 2 * D:(p + 1) * 2 * D],
                                      rhs[:, WD + p * 2 * D:WD + (p + 1) * 2 * D]], axis=1)
                tall = jnp.concatenate([rp, rp], axis=0)
                msk = ((lax.broadcasted_iota(jnp.int32, tall.shape, 0) // C)
                       == ((lax.broadcasted_iota(jnp.int32, tall.shape, 1) // D) % 2))
                uw_pairs.append(_dot(inv16[:, p * 2 * C:(p + 1) * 2 * C],
                                     jnp.where(msk, tall, jnp.zeros_like(tall))))
            uw = jnp.concatenate([up[:, :2 * D] for up in uw_pairs]
                                 + [up[:, 2 * D:] for up in uw_pairs], axis=1)
            u_s[par, c, :, cs] = uw[:, :WD]
            wq_s[par, c, :, cs] = jnp.concatenate([uw[:, WD:], q * eg], axis=0).astype(BF16)
            kd = k * head_cols(ekd_all, la)
            kd_t = jnp.concatenate(
                [jnp.concatenate([kd[:, (2 * p) * D:(2 * p + 1) * D],
                                  kd[:, (2 * p + 1) * D:(2 * p + 2) * D]], axis=0).T
                 for p in range(gw // 2)], axis=1)
            ak_s[par, c, :, h0 * C:(h0 + gw) * C] = jnp.concatenate(
                [raw[C:] * decay, kd_t], axis=0).astype(BF16)
            egl_s[par, c, :, cs] = head_cols(egl_all, la)
            yield

    def scans(first_chunk, par, wis):
        for c in range(unroll):
            n = first_chunk + c
            r0 = n * C if isinstance(n, int) else pl.multiple_of(n * C, C)
            for wi in wis:
                h0 = wi * gw
                cs = slice(h0 * D, (h0 + gw) * D)
                state = state_s[:, cs]
                sb = state.astype(BF16)
                r1 = jnp.concatenate(
                    [_dot(wq_s[par, c, :, h0 * D + p * 2 * D:h0 * D + (p + 1) * 2 * D],
                          jnp.where(
                              (lax.broadcasted_iota(jnp.int32, (2 * D, 2 * D), 0) // D)
                              == (lax.broadcasted_iota(jnp.int32, (2 * D, 2 * D), 1) // D),
                              jnp.concatenate([sb[:, p * 2 * D:(p + 1) * 2 * D]] * 2, axis=0),
                              jnp.zeros((2 * D, 2 * D), BF16)))
                     for p in range(gw // 2)], axis=1)
                v_new = u_s[par, c, :, cs] - r1[:C]
                yield
                r2 = _dot(ak_s[par, c, :, h0 * C:(h0 + gw) * C],
                          block_diag(v_new.astype(BF16), C, D))
                state_s[:, cs] = state * egl_s[par, c, 0:1, cs] + r2[C:]
                o = r1[C:] + r2[:C]
                gate = _silu(z_ref[pl.ds(r0, C), cs])
                for i in range(gw):
                    hs = slice(i * D, (i + 1) * D)
                    o_ref[pl.ds(r0, C), (h0 + i) * D:(h0 + i + 1) * D] = (
                        _rms(o[:, hs], ng_ref[...]) * gate[:, hs]).astype(o_ref.dtype)
                yield

    n_groups = nc // unroll
    def locals_of(first_chunk, par, first_tile_chunk=False):
        gens = []
        for c in range(unroll):
            terms = gate_terms(first_chunk + c)
            gens += [local(first_chunk + c, par, c, (wi,), terms, first=(first_tile_chunk and c == 0))
                     for wi in range(n_wide)]
        return gens

    def scans_of(first_chunk, par):
        return [scans(first_chunk, par, (wi,)) for wi in range(n_wide)]

    _run_interleaved(locals_of(0, 0, first_tile_chunk=True))

    def body(m, carry):
        _run_interleaved(scans_of(m * unroll, m % 2) + locals_of((m + 1) * unroll, (m + 1) % 2))
        return carry

    lax.fori_loop(0, n_groups - 1, body, 0)
    _run_interleaved(scans_of((n_groups - 1) * unroll, (n_groups - 1) % 2))

    prev_s[0] = q_ref[tile - SUBLANES:tile, :]
    prev_s[1] = k_ref[tile - SUBLANES:tile, :]
    prev_s[2] = v_ref[tile - SUBLANES:tile, :]


def _deltanet(proj, dn_conv_w, alog_row, dtb_row, dn_norm_g, batch, seq):
    D = DN_HEAD_DIM
    H = DN_HEADS
    hg = DN_GROUP
    ng = H // hg
    tile = DN_TILE
    nt = seq // tile
    wid = hg * D
    slots = DN_UNROLL
    col_blk = lambda off: pl.BlockSpec((tile, wid), lambda b, g, t, off=off: (b * nt + t, off + g))
    cw_blk = lambda off: pl.BlockSpec((DN_CONV_W, wid), lambda b, g, t, off=off: (0, off + g))
    gate_blk = 6 * 1024 // LANES
    return pl.pallas_call(
        functools.partial(_dn_kernel, tile=tile, hg=hg, gw=DN_WIDE, unroll=DN_UNROLL),
        grid=(batch, ng, nt),
        in_specs=[col_blk(0), col_blk(ng), col_blk(2 * ng), col_blk(3 * ng),
                  pl.BlockSpec((tile, LANES), lambda b, g, t: (b * nt + t, gate_blk + g)),
                  cw_blk(0), cw_blk(ng), cw_blk(2 * ng),
                  pl.BlockSpec((1, LANES), lambda b, g, t: (0, g)),
                  pl.BlockSpec((1, LANES), lambda b, g, t: (0, g)),
                  pl.BlockSpec((1, D), lambda b, g, t: (0, 0))],
        out_specs=pl.BlockSpec((tile, wid), lambda b, g, t: (b * nt + t, g)),
        out_shape=jax.ShapeDtypeStruct((batch * seq, H * D), BF16),
        scratch_shapes=[pltpu.VMEM((D, wid), F32),
                        pltpu.VMEM((3, SUBLANES, wid), F32),
                        pltpu.VMEM((2, slots, DN_CHUNK, wid), F32),
                        pltpu.VMEM((2, slots, 2 * DN_CHUNK, wid), BF16),
                        pltpu.VMEM((2, slots, DN_CHUNK + D, hg * DN_CHUNK), BF16),
                        pltpu.VMEM((2, slots, SUBLANES, wid), F32)],
        compiler_params=_params("parallel", "parallel", "arbitrary"),
        name="deltanet",
    )(proj, proj, proj, proj, proj, dn_conv_w, dn_conv_w, dn_conv_w,
      alog_row, dtb_row, dn_norm_g)


CF_HALO = 32
CF_ROWS = 128
CF_LN_ROWS = 32


def _conformer_kernel(a_ref, gt_ref, ah_ref, gh_ref, w_ref, b_ref, lg_ref, lb_ref, o_ref,
                      buf, ybuf, *, tt, ch):
    t = pl.program_id(1)
    halo = ah_ref[...] * _sigmoid(gh_ref[...])
    buf[0:CF_HALO, :] = jnp.where(t > 0, halo, 0.0)
    buf[CF_HALO:, :] = a_ref[...] * _sigmoid(gt_ref[...])

    win_rows = CF_ROWS + CF_HALO
    lead = CF_HALO - (CF_CONV_W - 1)

    def conv_cols(r0):
        for cb in range(ch // LANES):
            cs = slice(cb * LANES, (cb + 1) * LANES)
            win = buf[pl.ds(r0, win_rows), cs]
            acc = buf[pl.ds(r0 + CF_HALO, CF_ROWS), cs] * w_ref[CF_CONV_W - 1:CF_CONV_W, cs]
            for sub in range(SUBLANES):
                wb = win if sub == 0 else pltpu.roll(win, win_rows - sub, axis=0)
                for blk in range(CF_HALO // SUBLANES):
                    j = blk * SUBLANES + sub - lead
                    if 0 <= j < CF_CONV_W - 1:
                        acc = acc + wb[blk * SUBLANES:blk * SUBLANES + CF_ROWS, :] * w_ref[j:j + 1, cs]
            ybuf[pl.ds(r0, CF_ROWS), cs] = acc + b_ref[:, cs]
            yield

    def norm_rows(r0):
        y = ybuf[pl.ds(r0, CF_LN_ROWS), :]
        mu = jnp.mean(y, axis=-1, keepdims=True)
        yield
        yc = y - mu
        var = jnp.mean(yc * yc, axis=-1, keepdims=True)
        yield
        yn = yc * lax.rsqrt(var + EPS) * lg_ref[...] + lb_ref[...]
        yield
        o_ref[pl.ds(r0, CF_LN_ROWS), :] = _silu(yn).astype(o_ref.dtype)
        yield

    def norm_block(r0):
        return [norm_rows(r0 + s) for s in range(0, CF_ROWS, CF_LN_ROWS)]

    _run_interleaved([conv_cols(0)])

    def body(i, carry):
        r0 = pl.multiple_of(i * CF_ROWS, CF_ROWS)
        r_prev = pl.multiple_of((i - 1) * CF_ROWS, CF_ROWS)
        _run_interleaved(norm_block(r_prev) + [conv_cols(r0)])
        return carry

    lax.fori_loop(1, tt // CF_ROWS, body, 0)
    _run_interleaved(norm_block(tt - CF_ROWS))


def _conformer(proj, cf_dw_w, cf_dw_b, cf_ln_g, cf_ln_b, batch, seq, tt):
    ch = cf_dw_w.shape[1]
    nt = seq // tt
    a_col = 4 * 1024 // ch
    g_col = a_col + 1
    hb = tt // CF_HALO
    main = lambda c: pl.BlockSpec((tt, ch), lambda b, t, c=c: (b * nt + t, c))
    halo = lambda c: pl.BlockSpec(
        (CF_HALO, ch), lambda b, t, c=c: (jnp.maximum((b * nt + t) * hb - 1, 0), c))
    row = lambda r: pl.BlockSpec((r, ch), lambda b, t: (0, 0))
    return pl.pallas_call(
        functools.partial(_conformer_kernel, tt=tt, ch=ch),
        grid=(batch, nt),
        in_specs=[main(a_col), main(g_col), halo(a_col), halo(g_col),
                  row(CF_CONV_W), row(1), row(1), row(1)],
        out_specs=pl.BlockSpec((tt, ch), lambda b, t: (b * nt + t, 0)),
        out_shape=jax.ShapeDtypeStruct((batch * seq, ch), BF16),
        scratch_shapes=[pltpu.VMEM((tt + CF_HALO, ch), F32), pltpu.VMEM((tt, ch), F32)],
        compiler_params=_params("parallel", "parallel"),
        name="conformer",
    )(proj, proj, proj, proj, cf_dw_w, cf_dw_b, cf_ln_g, cf_ln_b)


def _out_proj_kernel(x_ref, o_ref, c_ref, wo_ref, wc_ref, h_ref):
    acc = _dot(o_ref[...].astype(BF16), wo_ref[...])
    acc = acc + _dot(c_ref[...].astype(BF16), wc_ref[...])
    h_ref[...] = x_ref[...] + acc


def _out_proj(x, o, c, w, tm):
    m, d = x.shape
    ko, kc = o.shape[1], c.shape[1]
    assert ko == kc and w.shape[0] == ko + kc
    w_rows = lambda blk: pl.BlockSpec((ko, d), lambda i, blk=blk: (blk, 0),
                                      pipeline_mode=pl.Buffered(1))
    return pl.pallas_call(
        _out_proj_kernel,
        grid=(m // tm,),
        in_specs=[pl.BlockSpec((tm, d), lambda i: (i, 0)),
                  pl.BlockSpec((tm, ko), lambda i: (i, 0)),
                  pl.BlockSpec((tm, kc), lambda i: (i, 0)),
                  w_rows(0), w_rows(1)],
        out_specs=pl.BlockSpec((tm, d), lambda i: (i, 0)),
        out_shape=jax.ShapeDtypeStruct((m, d), F32),
        compiler_params=_params("parallel"),
        name="out_proj",
    )(x, o, c, w, w)


def _xattn_kernel(h_ref, g_ref, wq_ref, kv_ref, wo_ref, o_ref, *, d_model):
    dh = d_model // XA_HEADS
    h1 = h_ref[...]
    hn = _rms(h1, g_ref[...]).astype(BF16)
    q = _dot(hn, wq_ref[...]).astype(BF16)
    heads = [None] * XA_HEADS

    def head(hd):
        cs = slice(hd * dh, (hd + 1) * dh)
        k = kv_ref[:, cs]
        v = kv_ref[:, d_model + hd * dh:d_model + (hd + 1) * dh]
        s = lax.dot_general(q[:, cs], k, (((1,), (1,)), ((), ())),
                            preferred_element_type=F32) * (dh ** -0.5)
        yield
        p = jnp.exp(s - jnp.max(s, axis=-1, keepdims=True))
        yield
        p = p / jnp.sum(p, axis=-1, keepdims=True)
        yield
        heads[hd] = _dot(p.astype(BF16), v).astype(BF16)
        yield

    _run_interleaved([head(hd) for hd in range(XA_HEADS)])
    o = jnp.concatenate(heads, axis=-1)
    o_ref[...] = h1 + _dot(o, wo_ref[...])


def _xattn(h1, g, wq, kv, wo, batch, seq, tm):
    m, d = h1.shape
    nt = seq // tm
    n_mem = kv.shape[0] // batch
    const = lambda shape: pl.BlockSpec(shape, lambda b, t: (0, 0), pipeline_mode=pl.Buffered(1))
    return pl.pallas_call(
        functools.partial(_xattn_kernel, d_model=d),
        grid=(batch, nt),
        in_specs=[pl.BlockSpec((tm, d), lambda b, t: (b * nt + t, 0)),
                  const((1, d)), const((d, d)),
                  pl.BlockSpec((n_mem, 2 * d), lambda b, t: (b, 0)),
                  const((d, d))],
        out_specs=pl.BlockSpec((tm, d), lambda b, t: (b * nt + t, 0)),
        out_shape=jax.ShapeDtypeStruct((m, d), F32),
        compiler_params=_params("parallel", "parallel"),
        name="xattn",
    )(h1, g, wq, kv, wo)


def _mlp_kernel(h_ref, g_ref, w1_ref, w2_ref, gf_ref, o_ref, hn_s, acc_s):
    j = pl.program_id(1)

    @pl.when(j == 0)
    def _():
        hn_s[...] = _rms(h_ref[...], g_ref[...]).astype(BF16)
        acc_s[...] = jnp.zeros_like(acc_s)

    hid = jnp.maximum(_dot(hn_s[...], w1_ref[...]), 0.0)
    acc_s[...] += _dot((hid * hid).astype(BF16), w2_ref[...])

    @pl.when(j == pl.num_programs(1) - 1)
    def _():
        o_ref[...] = _rms(h_ref[...] + acc_s[...], gf_ref[...])


def _mlp(h2, g, w1, w2, gf, tm, tf):
    m, d = h2.shape
    ff = w1.shape[1]
    return pl.pallas_call(
        _mlp_kernel,
        grid=(m // tm, ff // tf),
        in_specs=[pl.BlockSpec((tm, d), lambda i, j: (i, 0)),
                  pl.BlockSpec((1, d), lambda i, j: (0, 0)),
                  pl.BlockSpec((d, tf), lambda i, j: (0, j)),
                  pl.BlockSpec((tf, d), lambda i, j: (j, 0)),
                  pl.BlockSpec((1, d), lambda i, j: (0, 0))],
        out_specs=pl.BlockSpec((tm, d), lambda i, j: (i, 0)),
        out_shape=jax.ShapeDtypeStruct((m, d), F32),
        scratch_shapes=[pltpu.VMEM((tm, d), BF16), pltpu.VMEM((tm, d), F32)],
        compiler_params=_params("parallel", "arbitrary"),
        name="mlp",
    )(h2, g, w1, w2, gf)


def kernel(x, mem, norm_mix_g, w_in, dn_conv_w, dn_a_log, dn_dt_bias, dn_norm_g, cf_dw_w, cf_dw_b, cf_ln_g, cf_ln_b, w_out, norm_xa_g, norm_mem_g, xa_wq, xa_wk, xa_wv, xa_wo, norm_mlp_g, mlp_w1, mlp_w2, norm_final_g):
    batch, seq, d = x.shape
    n_mem = mem.shape[1]
    d_delta = DN_HEADS * DN_HEAD_DIM
    assert w_in.shape[0] == 1, "single-layer block"
    row = lambda v: v.reshape(1, -1).astype(F32)

    w_in0 = w_in[0]
    n_gate = 2 * DN_HEADS
    hg = DN_GROUP
    n_grp = DN_HEADS // hg
    w_beta = w_in0[:, 4 * d_delta:4 * d_delta + DN_HEADS]
    w_alpha = w_in0[:, 4 * d_delta + DN_HEADS:4 * d_delta + n_gate]
    gate_cols = []
    for grp in range(n_grp):
        hs = slice(grp * hg, (grp + 1) * hg)
        gate_cols += [w_beta[:, hs], w_alpha[:, hs], jnp.zeros((d, LANES - 2 * hg), F32)]
    gate_cols += [jnp.zeros((d, 2 * LANES - n_grp * LANES), F32)] * (n_grp < 2)
    w_gate = jnp.concatenate(gate_cols, axis=1).astype(BF16)
    w_proj = _layout_w_in(w_in0.astype(BF16), w_gate, n_lead=4 * d_delta, n_skip=n_gate,
                          n_tail=2 * d_delta)

    def pad_gate(vec):
        v2 = vec.astype(F32).reshape(n_grp, hg)
        return jnp.pad(v2, ((0, 0), (hg, LANES - 2 * hg))).reshape(1, n_grp * LANES)

    x2 = x.reshape(batch * seq, d)
    proj = _norm_matmul(x2, row(norm_mix_g[0]), w_proj, tm=1024, tn=1280)

    o = _deltanet(proj, dn_conv_w[0], pad_gate(dn_a_log[0]), pad_gate(dn_dt_bias[0]),
                  row(dn_norm_g[0]), batch, seq)
    c = _conformer(proj, cf_dw_w[0], row(cf_dw_b[0]), row(cf_ln_g[0]), row(cf_ln_b[0]),
                   batch, seq, tt=1024)

    h1 = _out_proj(x2, o, c, w_out[0].astype(BF16), tm=512)

    w_kv = jnp.concatenate([xa_wk[0].astype(BF16), xa_wv[0].astype(BF16)], axis=1)
    kv = _norm_matmul(mem.reshape(batch * n_mem, d), row(norm_mem_g[0]), w_kv,
                      tm=512, tn=2048, out_dtype=BF16)
    h2 = _xattn(h1, row(norm_xa_g[0]), xa_wq[0].astype(BF16), kv, xa_wo[0].astype(BF16),
                batch, seq, tm=512)

    out = _mlp(h2, row(norm_mlp_g[0]), mlp_w1[0].astype(BF16), mlp_w2[0].astype(BF16),
               row(norm_final_g), tm=512, tf=1024)
    return out.reshape(batch, seq, d)
```
